```python
import jax, jax.numpy as jnp
from jax import lax
import numpy as np

D_MODEL = 1024
BATCH = 2
SEQ = 8192
DEPTH = 4
DEC_BATCH = 32
DEC_SEQ = 4
PAST_LEN = 8192
PAGE_SIZE = 128

HEAD_DIM = 64
A_W = D_MODEL // 4
B_W = D_MODEL // 2
C_W = D_MODEL // 4
A_HEADS = A_W // HEAD_DIM
B_HEADS = B_W // HEAD_DIM
C_HEADS = C_W // HEAD_DIM
MIX_W = A_W + B_W + C_W
N_KV = 2
KV_W = N_KV * HEAD_DIM
IDX_HEADS = 8
IDX_DIM = 64
TOPK_MAX = 256
ROT_DIM = HEAD_DIM // 4
ROPE_THETA = 500000.0
CONV_W = 3
CHUNK = 128
D_FF = 2816
Q_BLOCK = 128
LN_EPS = 1e-5
ALPHA = (2 * DEPTH) ** 0.25
BETA = (8 * DEPTH) ** -0.25
IN_SIZES = (A_W, A_W, A_W, B_W, KV_W, KV_W, IDX_HEADS * IDX_DIM, IDX_DIM, IDX_HEADS, C_W, C_W)
N_IN = sum(IN_SIZES)

kernel_name = "hybrid_conv_dsa_gmlp_decoder_step"

F32 = jnp.float32


def _layer_norm(x, g, b):
    xf = x.astype(F32)
    mu = jnp.mean(xf, axis=-1, keepdims=True)
    xc = xf - mu
    var = jnp.mean(xc * xc, axis=-1, keepdims=True)
    return (xc * lax.rsqrt(var + LN_EPS) * g.astype(F32) + b.astype(F32)).astype(x.dtype)


def _rotary(x, pos):
    half = ROT_DIM // 2
    inv = ROPE_THETA ** (-jnp.arange(half, dtype=F32) / half)
    ang = pos.astype(F32)[:, None] * inv[None, :]
    cos = jnp.cos(ang)[:, None, :]
    sin = jnp.sin(ang)[:, None, :]
    xr = x[..., :ROT_DIM].astype(F32)
    x1, x2 = xr[..., :half], xr[..., half:]
    rot = jnp.concatenate([x1 * cos - x2 * sin, x2 * cos + x1 * sin], axis=-1)
    return jnp.concatenate([rot.astype(x.dtype), x[..., ROT_DIM:]], axis=-1)


def _causal_dwconv(z, prev, w):
    T = z.shape[1]
    zz = jnp.concatenate([prev.astype(z.dtype), z], axis=1)
    y = zz[:, 0:T] * w[0]
    for i in range(1, CONV_W):
        y = y + zz[:, i:i + T] * w[i]
    return y, zz[:, -(CONV_W - 1):]


def _project(x, pos, w):
    h = x @ w
    offs = np.cumsum(IN_SIZES)[:-1].tolist()
    gb, gc, ha, q, k, v, qi, ki, wi, u, vc = jnp.split(h, offs, axis=-1)
    Bx, T = x.shape[:2]
    q = _rotary(q.reshape(Bx, T, B_HEADS, HEAD_DIM), pos)
    k = _rotary(k.reshape(Bx, T, N_KV, HEAD_DIM), pos)
    v = v.reshape(Bx, T, N_KV, HEAD_DIM)
    qi = _rotary(qi.reshape(Bx, T, IDX_HEADS, IDX_DIM), pos)
    ki = _rotary(ki.reshape(Bx, T, 1, IDX_DIM), pos)[:, :, 0]
    wi = wi * (IDX_HEADS ** -0.5)
    return gb, gc, ha, q, k, v, qi, ki, wi, u, vc


def _short_conv(gb, gc, ha, prev, w):
    y, new_prev = _causal_dwconv(gc * ha, prev, w)
    return gb * y, new_prev


def _index_scores(qi, wi, ki):
    dots = jnp.einsum('bqhd,bsd->bqhs', qi.astype(F32), ki.astype(F32)) * (IDX_DIM ** -0.5)
    return jnp.einsum('bqhs,bqh->bqs', jax.nn.relu(dots), wi.astype(F32))


def _attend(q, kg, vg, valid):
    Bq, Q = q.shape[:2]
    qg = q.reshape(Bq, Q, N_KV, B_HEADS // N_KV, HEAD_DIM).astype(F32)
    s = jnp.einsum('bqgrd,bqkgd->bqgrk', qg, kg.astype(F32)) * (HEAD_DIM ** -0.5)
    s = jnp.where(valid[:, :, None, None, :], s, -jnp.inf)
    p = jax.nn.softmax(s, axis=-1)
    o = jnp.einsum('bqgrk,bqkgd->bqgrd', p, vg.astype(F32))
    return o.reshape(Bq, Q, B_W).astype(q.dtype)


def _gather_rows(a, idx):
    return jax.vmap(lambda aa, ii: aa[ii])(a, idx)


def _dsa_prompt(q, k, v, qi, wi, ki):
    Bp, S = q.shape[:2]
    ksel = min(TOPK_MAX, S // 4)
    nblk = S // Q_BLOCK
    spos = jnp.arange(S, dtype=jnp.int32)

    def to_blocks(a):
        return jnp.moveaxis(a.reshape(Bp, nblk, Q_BLOCK, *a.shape[2:]), 1, 0)

    def one_block(args):
        qb, qib, wib, t0 = args
        tpos = t0 + jnp.arange(Q_BLOCK, dtype=jnp.int32)
        sc = _index_scores(qib, wib, ki)
        sc = jnp.where(spos[None, None, :] <= tpos[None, :, None], sc, -jnp.inf)
        _, idx = lax.top_k(sc, ksel)
        valid = idx <= tpos[None, :, None]
        return _attend(qb, _gather_rows(k, idx), _gather_rows(v, idx), valid)

    t0s = jnp.arange(nblk, dtype=jnp.int32) * Q_BLOCK
    out = lax.map(one_block, (to_blocks(q), to_blocks(qi), to_blocks(wi), t0s))
    return jnp.moveaxis(out, 0, 1).reshape(Bp, S, B_W)


def _dsa_sample(q, k_new, v_new, qi, wi, ki_new, pool_k, pool_v, pool_ki, page_table):
    Bd, T = q.shape[:2]
    n_pages = page_table.shape[1]
    past = n_pages * PAGE_SIZE
    L = past + T
    ksel = min(TOPK_MAX, L // 4)
    ki_past = pool_ki[page_table].reshape(Bd, past, IDX_DIM)
    ki_all = jnp.concatenate([ki_past.astype(ki_new.dtype), ki_new], axis=1)
    tpos = past + jnp.arange(T, dtype=jnp.int32)
    sc = _index_scores(qi, wi, ki_all)
    sc = jnp.where(jnp.arange(L, dtype=jnp.int32)[None, None, :] <= tpos[None, :, None], sc, -jnp.inf)
    _, idx = lax.top_k(sc, ksel)
    valid = idx <= tpos[None, :, None]
    in_past = (idx < past)[..., None, None]
    page = jnp.minimum(idx // PAGE_SIZE, n_pages - 1)
    phys = jax.vmap(lambda pt, pg: pt[pg])(page_table, page)
    off = idx % PAGE_SIZE
    new_i = jnp.clip(idx - past, 0, T - 1)
    kg = jnp.where(in_past, pool_k[phys, off].astype(k_new.dtype), _gather_rows(k_new, new_i))
    vg = jnp.where(in_past, pool_v[phys, off].astype(v_new.dtype), _gather_rows(v_new, new_i))
    return _attend(q, kg, vg, valid)


def _chunk_mlp(u, vc, g, b, ws, bs):
    vn = _layer_norm(vc, g, b)
    Bx, T = u.shape[:2]
    n = min(T, CHUNK)
    mask = jnp.tril(jnp.ones((n, n), dtype=bool))
    wm = jnp.where(mask[None], ws[:, :n, :n], 0)
    vr = vn.reshape(Bx, T // n, n, C_HEADS, HEAD_DIM)
    mixed = jnp.einsum('hts,bcshd->bcthd', wm, vr) + jnp.transpose(bs[:, :n])[None, None, :, :, None]
    return u * mixed.reshape(Bx, T, C_W), vn


def _conv_ffn(x, prev, w_up, w_conv, w_down):
    h, new_prev = _causal_dwconv(x @ w_up, prev, w_conv)
    g, u = jnp.split(h, 2, axis=-1)
    return (jax.nn.silu(g) * u) @ w_down, new_prev


def setup_inputs(seed: int = 0) -> dict:
    key = jax.random.key(seed)
    ks = jax.random.split(key, 24)
    n_pages = PAST_LEN // PAGE_SIZE
    n_pool = (DEC_BATCH * n_pages * 5) // 4
    nrm = jax.random.normal
    page_table = jax.random.permutation(ks[0], n_pool)[:DEC_BATCH * n_pages].reshape(DEC_BATCH, n_pages).astype(jnp.int32)
    return {
        "x_prompt": nrm(ks[1], (BATCH, SEQ, D_MODEL), F32),
        "x_sample": nrm(ks[2], (DEC_BATCH, DEC_SEQ, D_MODEL), F32),
        "cache_k": nrm(ks[3], (DEPTH, n_pool, PAGE_SIZE, N_KV, HEAD_DIM), F32),
        "cache_v": nrm(ks[4], (DEPTH, n_pool, PAGE_SIZE, N_KV, HEAD_DIM), F32),
        "cache_kidx": nrm(ks[5], (DEPTH, n_pool, PAGE_SIZE, IDX_DIM), F32),
        "page_table": page_table,
        "state_conv": nrm(ks[6], (DEPTH, DEC_BATCH, CONV_W - 1, A_W), F32),
        "state_ffn_conv": nrm(ks[7], (DEPTH, DEC_BATCH, CONV_W - 1, 2 * D_FF), F32),
        "w_in": nrm(ks[8], (DEPTH, D_MODEL, N_IN), F32) * D_MODEL ** -0.5,
        "conv_a": nrm(ks[9], (DEPTH, CONV_W, A_W), F32) * 0.5,
        "ln_cv_g": 1.0 + 0.02 * nrm(ks[10], (DEPTH, C_W), F32),
        "ln_cv_b": 0.02 * nrm(ks[11], (DEPTH, C_W), F32),
        "w_sp": nrm(ks[12], (DEPTH, C_HEADS, CHUNK, CHUNK), F32) * CHUNK ** -0.5,
        "b_sp": 1.0 + 0.02 * nrm(ks[13], (DEPTH, C_HEADS, CHUNK), F32),
        "w_o": nrm(ks[14], (DEPTH, MIX_W, D_MODEL), F32) * (MIX_W ** -0.5) * BETA,
        "ln1_g": 1.0 + 0.02 * nrm(ks[15], (DEPTH, D_MODEL), F32),
        "ln1_b": 0.02 * nrm(ks[16], (DEPTH, D_MODEL), F32),
        "w_up": nrm(ks[17], (DEPTH, D_MODEL, 2 * D_FF), F32) * D_MODEL ** -0.5,
        "conv_f": nrm(ks[18], (DEPTH, CONV_W, 2 * D_FF), F32) * 0.5,
        "w_down": nrm(ks[19], (DEPTH, D_FF, D_MODEL), F32) * (D_FF ** -0.5) * BETA,
        "ln2_g": 1.0 + 0.02 * nrm(ks[20], (DEPTH, D_MODEL), F32),
        "ln2_b": 0.02 * nrm(ks[21], (DEPTH, D_MODEL), F32),
    }


def reference(x_prompt, x_sample, cache_k, cache_v, cache_kidx, page_table, state_conv, state_ffn_conv,
              w_in, conv_a, ln_cv_g, ln_cv_b, w_sp, b_sp, w_o, ln1_g, ln1_b, w_up, conv_f, w_down, ln2_g, ln2_b):
    Bp, Sp = x_prompt.shape[:2]
    Bd, Td = x_sample.shape[:2]
    past = page_table.shape[1] * PAGE_SIZE
    pos_p = jnp.arange(Sp, dtype=jnp.int32)
    pos_s = past + jnp.arange(Td, dtype=jnp.int32)
    hp, hs = x_prompt, x_sample
    kp_l, ks_l, vp_l, vs_l, kip_l, kis_l = [], [], [], [], [], []
    cp_l, cs_l, fp_l, fs_l, chs_l = [], [], [], [], []
    for l in range(DEPTH):
        gb, gc, ha, q, k, v, qi, ki, wi, u, vc = _project(hp, pos_p, w_in[l])
        a_out, conv_new = _short_conv(gb, gc, ha, jnp.zeros((Bp, CONV_W - 1, A_W), hp.dtype), conv_a[l])
        b_out = _dsa_prompt(q, k, v, qi, wi, ki)
        c_out, _ = _chunk_mlp(u, vc, ln_cv_g[l], ln_cv_b[l], w_sp[l], b_sp[l])
        mix = jnp.concatenate([a_out, b_out, c_out], axis=-1) @ w_o[l]
        hp = _layer_norm(ALPHA * hp + mix, ln1_g[l], ln1_b[l])
        f_out, ffn_new = _conv_ffn(hp, jnp.zeros((Bp, CONV_W - 1, 2 * D_FF), hp.dtype), w_up[l], conv_f[l], w_down[l])
        hp = _layer_norm(ALPHA * hp + f_out, ln2_g[l], ln2_b[l])
        kp_l.append(k); vp_l.append(v); kip_l.append(ki); cp_l.append(conv_new); fp_l.append(ffn_new)

        gb, gc, ha, q, k, v, qi, ki, wi, u, vc = _project(hs, pos_s, w_in[l])
        a_out, conv_new = _short_conv(gb, gc, ha, state_conv[l], conv_a[l])
        b_out = _dsa_sample(q, k, v, qi, wi, ki, cache_k[l], cache_v[l], cache_kidx[l], page_table)
        c_out, vn = _chunk_mlp(u, vc, ln_cv_g[l], ln_cv_b[l], w_sp[l], b_sp[l])
        mix = jnp.concatenate([a_out, b_out, c_out], axis=-1) @ w_o[l]
        hs = _layer_norm(ALPHA * hs + mix, ln1_g[l], ln1_b[l])
        f_out, ffn_new = _conv_ffn(hs, state_ffn_conv[l], w_up[l], conv_f[l], w_down[l])
        hs = _layer_norm(ALPHA * hs + f_out, ln2_g[l], ln2_b[l])
        ks_l.append(k); vs_l.append(v); kis_l.append(ki); cs_l.append(conv_new); fs_l.append(ffn_new); chs_l.append(vn)
    return (hp, hs,
            jnp.stack(kp_l), jnp.stack(ks_l),
            jnp.stack(vp_l), jnp.stack(vs_l),
            jnp.stack(kip_l), jnp.stack(kis_l),
            jnp.stack(cp_l), jnp.stack(cs_l),
            jnp.stack(fp_l), jnp.stack(fs_l),
            jnp.stack(chs_l))
```

```python
import functools

import jax
import jax.numpy as jnp
import numpy as np
from jax import lax
from jax.experimental import pallas as pl
from jax.experimental.pallas import tpu as pltpu

F32 = jnp.float32
BF16 = jnp.bfloat16
I32 = jnp.int32

D_MODEL = 1024
HEAD_DIM = 64
A_W = D_MODEL // 4
B_W = D_MODEL // 2
C_W = D_MODEL // 4
B_HEADS = B_W // HEAD_DIM
C_HEADS = C_W // HEAD_DIM
N_KV = 2
KV_W = N_KV * HEAD_DIM
GROUP = B_HEADS // N_KV
IDX_HEADS = 8
IDX_DIM = 64
TOPK_MAX = 256
ROT_DIM = HEAD_DIM // 4
ROT_HALF = ROT_DIM // 2
ROPE_THETA = 500000.0
CONV_W = 3
CHUNK = 128
D_FF = 2816
PAGE_SIZE = 128
LN_EPS = 1e-5

LANES = 128
SUBLANES = 8
VMEM_LIMIT = 56 * 1024 * 1024

OFF_GB = 0
OFF_GC = OFF_GB + A_W
OFF_HA = OFF_GC + A_W
OFF_Q = OFF_HA + A_W
OFF_K = OFF_Q + B_W
OFF_V = OFF_K + KV_W
OFF_QI = OFF_V + KV_W
OFF_KI = OFF_QI + IDX_HEADS * IDX_DIM
OFF_WI = OFF_KI + LANES
OFF_U = OFF_WI + LANES
OFF_VC = OFF_U + C_W
N_IN_PAD = OFF_VC + C_W

NEG_BIG = -0.7 * float(np.finfo(np.float32).max)
INT_MIN = -(2 ** 31)
NEG_INF_KEY = int(np.int32(np.uint32(0xFF800000) ^ np.uint32(0x7FFFFFFF)))

_NT = (((1,), (1,)), ((), ()))


def _vmem_full():
    return pl.BlockSpec(memory_space=pltpu.VMEM)


def _layer_norm(x, g, b):
    mu = jnp.mean(x, axis=-1, keepdims=True)
    xc = x - mu
    var = jnp.mean(xc * xc, axis=-1, keepdims=True)
    return xc * lax.rsqrt(var + LN_EPS) * g + b


def _sortable_key(x):
    k = lax.bitcast_convert_type(x, I32)
    return k ^ (lax.shift_right_arithmetic(k, 31) & 0x7FFFFFFF)


def _rotary(seg, cos, sin, first_half):
    n = seg.shape[-1]
    fwd = pltpu.roll(seg, n - ROT_HALF, 1)
    bwd = pltpu.roll(seg, ROT_HALF, 1)
    return seg * cos + jnp.where(first_half, fwd, bwd) * sin


def _causal_conv3(cur, hist_ref, cols, w_ref, m, state=None):
    z1 = hist_ref[pl.ds(SUBLANES - 1, m), cols]
    z2 = hist_ref[pl.ds(SUBLANES - 2, m), cols]
    if state is not None:
        t, p1, p2 = state
        z1 = jnp.where(t >= 1, z1, p1)
        z2 = jnp.where(t >= 2, z2, p2)
    return z2 * w_ref[0:1, cols] + z1 * w_ref[1:2, cols] + cur * w_ref[2:3, cols]


def _proj_body(*refs, m, stateful):
    if stateful:
        (x_ref, w_ref, cos_ref, sin_ref, conva_ref, lng_ref, lnb_ref, wm_ref, bsr_ref, p1_ref, p2_ref,
         a_ref, c_ref, qh_ref, qih_ref, wi_ref, kf_ref, vf_ref, kif_ref, kg_ref, vg_ref, kib_ref, z_ref, vn_ref,
         hist_ref) = refs
    else:
        (x_ref, w_ref, cos_ref, sin_ref, conva_ref, lng_ref, lnb_ref, wm_ref, bsr_ref,
         a_ref, c_ref, qh_ref, qih_ref, wi_ref, kf_ref, vf_ref, kif_ref, kg_ref, vg_ref, kib_ref, z_ref,
         hist_ref) = refs

    first_tile = pl.program_id(1) == 0

    @pl.when(first_tile)
    def _():
        hist_ref[0:SUBLANES, :] = jnp.zeros((SUBLANES, A_W), F32)

    x = x_ref[0].astype(BF16)
    h = jnp.dot(x, w_ref[...], preferred_element_type=F32)

    z = h[:, OFF_GC:OFF_GC + A_W] * h[:, OFF_HA:OFF_HA + A_W]
    hist_ref[pl.ds(SUBLANES, m), :] = z
    state = None
    if stateful:
        t = lax.broadcasted_iota(I32, (m, 1), 0) % stateful
        state = (t, p1_ref[...], p2_ref[...])
    y = _causal_conv3(z, hist_ref, slice(None), conva_ref, m, state)
    a_ref[0] = (h[:, OFF_GB:OFF_GB + A_W] * y).astype(BF16)
    if stateful:
        z_ref[0] = z
    else:
        z_ref[0] = z[m - 2:m, :]
        hist_ref[SUBLANES - 2:SUBLANES, :] = z[m - 2:m, :]

    vn = _layer_norm(h[:, OFF_VC:OFF_VC + C_W], lng_ref[...], lnb_ref[...])
    if stateful:
        vn_ref[0] = vn
    vnb = vn.astype(BF16)
    lane_head = lax.broadcasted_iota(I32, (CHUNK, C_W), 1) // HEAD_DIM
    for r in range(m // CHUNK):
        rows = slice(r * CHUNK, (r + 1) * CHUNK)
        mixed = jnp.zeros((CHUNK, C_W), F32)
        for hh in range(C_HEADS):
            part = jnp.dot(wm_ref[hh], vnb[rows], preferred_element_type=F32)
            mixed = jnp.where(lane_head == hh, part, mixed)
        c_ref[0, rows, :] = (h[rows, OFF_U:OFF_U + C_W] * (mixed + bsr_ref[...])).astype(BF16)

    cos = cos_ref[...]
    sin = sin_ref[...]
    first_half = (lax.broadcasted_iota(I32, (m, LANES), 1) % HEAD_DIM) < ROT_HALF

    def rot(off):
        return _rotary(h[:, off:off + LANES], cos, sin, first_half)

    for j in range(B_W // LANES):
        qr = (rot(OFF_Q + j * LANES) * (HEAD_DIM ** -0.5)).astype(BF16)
        qir = (rot(OFF_QI + j * LANES) * (IDX_DIM ** -0.5)).astype(BF16)
        for e in range(LANES // HEAD_DIM):
            hh = j * (LANES // HEAD_DIM) + e
            qh_ref[0, hh] = qr[:, e * HEAD_DIM:(e + 1) * HEAD_DIM]
            qih_ref[0, hh] = qir[:, e * HEAD_DIM:(e + 1) * HEAD_DIM]
    kr = rot(OFF_K)
    v = h[:, OFF_V:OFF_V + KV_W]
    kf_ref[0] = kr
    vf_ref[0] = v
    for g in range(N_KV):
        kg_ref[0, g] = kr[:, g * HEAD_DIM:(g + 1) * HEAD_DIM].astype(BF16)
        vg_ref[0, g] = v[:, g * HEAD_DIM:(g + 1) * HEAD_DIM].astype(BF16)
    kir = rot(OFF_KI)[:, :IDX_DIM]
    kif_ref[0] = kir
    kib_ref[0] = kir.astype(BF16)
    wi_ref[0] = h[:, OFF_WI:OFF_WI + IDX_HEADS] * (IDX_HEADS ** -0.5)


def _proj_call(x, w, cos, sin, conva, lng, lnb, wm, bsr, p1=None, p2=None, *, tm, t_dec=0):
    bsz, tlen, _ = x.shape
    stateful = t_dec
    nt = tlen // tm
    grid = (bsz, nt)

    def row_blk(width):
        return pl.BlockSpec((1, tm, width), lambda b, j: (b, j, 0))

    def head_blk(nh):
        return pl.BlockSpec((1, nh, tm, HEAD_DIM), lambda b, j: (b, 0, j, 0))

    tab = pl.BlockSpec((tm, LANES), lambda b, j: (j, 0))
    in_specs = [row_blk(D_MODEL), _vmem_full(), tab, tab] + [_vmem_full()] * 5
    args = [x, w, cos, sin, conva, lng, lnb, wm, bsr]
    if stateful:
        in_specs += [_vmem_full(), _vmem_full()]
        args += [p1, p2]
    out_shape = [
        jax.ShapeDtypeStruct((bsz, tlen, A_W), BF16),
        jax.ShapeDtypeStruct((bsz, tlen, C_W), BF16),
        jax.ShapeDtypeStruct((bsz, B_HEADS, tlen, HEAD_DIM), BF16),
        jax.ShapeDtypeStruct((bsz, IDX_HEADS, tlen, IDX_DIM), BF16),
        jax.ShapeDtypeStruct((bsz, tlen, IDX_HEADS), F32),
        jax.ShapeDtypeStruct((bsz, tlen, KV_W), F32),
        jax.ShapeDtypeStruct((bsz, tlen, KV_W), F32),
        jax.ShapeDtypeStruct((bsz, tlen, IDX_DIM), F32),
        jax.ShapeDtypeStruct((bsz, N_KV, tlen, HEAD_DIM), BF16),
        jax.ShapeDtypeStruct((bsz, N_KV, tlen, HEAD_DIM), BF16),
        jax.ShapeDtypeStruct((bsz, tlen, IDX_DIM), BF16),
    ]
    out_specs = [row_blk(A_W), row_blk(C_W), head_blk(B_HEADS), head_blk(IDX_HEADS), row_blk(IDX_HEADS),
                 row_blk(KV_W), row_blk(KV_W), row_blk(IDX_DIM), head_blk(N_KV), head_blk(N_KV), row_blk(IDX_DIM)]
    if stateful:
        out_shape += [jax.ShapeDtypeStruct((bsz, tlen, A_W), F32), jax.ShapeDtypeStruct((bsz, tlen, C_W), F32)]
        out_specs += [row_blk(A_W), row_blk(C_W)]
    else:
        out_shape += [jax.ShapeDtypeStruct((bsz, CONV_W - 1, A_W), F32)]
        out_specs += [pl.BlockSpec((1, CONV_W - 1, A_W), lambda b, j: (b, 0, 0))]
    return pl.pallas_call(
        functools.partial(_proj_body, m=tm, stateful=stateful),
        grid=grid,
        in_specs=in_specs,
        out_specs=out_specs,
        out_shape=out_shape,
        scratch_shapes=[pltpu.VMEM((SUBLANES + tm, A_W), F32)],
        compiler_params=pltpu.CompilerParams(
            dimension_semantics=("arbitrary", "arbitrary"), vmem_limit_bytes=VMEM_LIMIT),
        name="proj_dec" if stateful else "proj",
    )(*args)


def _kth_largest_key(count_ge, ksel, shape):
    thr = jnp.where(count_ge(jnp.zeros(shape, I32)) >= ksel, 0, INT_MIN).astype(I32)

    def bit_step(it, thr):
        cand = thr + lax.shift_left(jnp.int32(1), 30 - it)
        return jnp.where(count_ge(cand) >= ksel, cand, thr)

    return lax.fori_loop(0, 31, bit_step, thr)


def _tie_cutoff(count_tied_below, need, nbits, shape):
    def bit_step(it, cut):
        cand = cut + lax.shift_left(jnp.int32(1), nbits - 1 - it)
        return jnp.where(count_tied_below(cand) <= need - 1, cand, cut)

    return lax.fori_loop(0, nbits, bit_step, jnp.zeros(shape, I32))


def _dsa_prompt_body(qi_ref, w_ref, ki_ref, q_ref, k_ref, v_ref, o_ref, sk_ref, m_ref, l_ref, acc_ref,
                     *, qb, ck, ksel, idx_bits):
    i = pl.program_id(1)
    nck = (i * qb) // ck + 1
    rows1 = (qb, 1)

    qi_all = qi_ref[0].reshape(IDX_HEADS * qb, IDX_DIM)
    w = w_ref[0]
    diff = lax.broadcasted_iota(I32, (qb, ck), 1) - lax.broadcasted_iota(I32, (qb, ck), 0)

    def score_chunk(c, carry):
        off = pl.multiple_of(c * ck, ck)
        d = lax.dot_general(qi_all, ki_ref[0, pl.ds(off, ck), :], _NT, preferred_element_type=F32)
        sc = jnp.zeros((qb, ck), F32)
        for hh in range(IDX_HEADS):
            sc = sc + jnp.maximum(d[hh * qb:(hh + 1) * qb], 0.0) * w[:, hh:hh + 1]
        sc = jnp.where(diff <= i * qb - off, sc, -jnp.inf)
        sk_ref[:, pl.ds(off, ck)] = _sortable_key(sc)
        return carry

    lax.fori_loop(0, nck, score_chunk, 0)

    def fold(cnt):
        acc = cnt[:, 0:LANES]
        for j in range(1, ck // LANES):
            acc = acc + cnt[:, j * LANES:(j + 1) * LANES]
        return acc

    def count_rows(pred):
        def body(c, acc):
            off = pl.multiple_of(c * ck, ck)
            return acc + fold(jnp.where(pred(sk_ref[:, pl.ds(off, ck)], off), 1, 0))

        acc = lax.fori_loop(0, nck, body, jnp.zeros((qb, LANES), I32))
        return jnp.sum(acc, axis=1, keepdims=True)

    def count_ge(cand):
        return count_rows(lambda blk, off: blk >= cand)

    thr = _kth_largest_key(count_ge, ksel, rows1)

    n_ge = count_ge(thr)
    tie_rows = (n_ge > ksel) & (thr > NEG_INF_KEY)

    @pl.when(jnp.max(tie_rows.astype(I32)) > 0)
    def _():
        need = ksel - count_ge(thr + 1)
        lane = lax.broadcasted_iota(I32, (qb, ck), 1)

        def count_tied_below(cand):
            return count_rows(lambda blk, off: (blk == thr) & (lane + off < cand))

        cut = _tie_cutoff(count_tied_below, need, idx_bits, rows1)

        def demote(c, carry):
            off = pl.multiple_of(c * ck, ck)
            blk = sk_ref[:, pl.ds(off, ck)]
            sk_ref[:, pl.ds(off, ck)] = jnp.where((blk == thr) & (lane + off > cut), blk - 1, blk)
            return carry

        lax.fori_loop(0, nck, demote, 0)

    thr_sel = jnp.maximum(thr, NEG_INF_KEY + 1)

    m_ref[...] = jnp.full(m_ref.shape, NEG_BIG, F32)
    l_ref[...] = jnp.zeros(l_ref.shape, F32)
    acc_ref[...] = jnp.zeros(acc_ref.shape, F32)

    def attend_chunk(c, carry):
        off = pl.multiple_of(c * ck, ck)
        bias = jnp.where(sk_ref[:, pl.ds(off, ck)] >= thr_sel, 0.0, NEG_BIG)
        for g in range(N_KV):
            qg = q_ref[0, g * GROUP:(g + 1) * GROUP].reshape(GROUP * qb, HEAD_DIM)
            s = lax.dot_general(qg, k_ref[0, g, pl.ds(off, ck), :], _NT, preferred_element_type=F32)
            s = (s.reshape(GROUP, qb, ck) + bias[None]).reshape(GROUP * qb, ck)
            m_old = m_ref[g]
            m_new = jnp.maximum(m_old, jnp.max(s, axis=1, keepdims=True))
            alpha = jnp.exp(m_old - m_new)
            p = jnp.exp(s - m_new)
            l_ref[g] = alpha * l_ref[g] + jnp.sum(p, axis=1, keepdims=True)
            acc_ref[g] = alpha * acc_ref[g] + jnp.dot(
                p.astype(BF16), v_ref[0, g, pl.ds(off, ck), :], preferred_element_type=F32)
            m_ref[g] = m_new
        return carry

    lax.fori_loop(0, nck, attend_chunk, 0)

    for g in range(N_KV):
        o = acc_ref[g] / l_ref[g]
        for r in range(GROUP):
            hh = g * GROUP + r
            o_ref[0, :, hh * HEAD_DIM:(hh + 1) * HEAD_DIM] = o[r * qb:(r + 1) * qb].astype(o_ref.dtype)


def _dsa_prompt_call(qih, wi, kib, qh, kg, vg):
    bsz, _, slen, _ = qh.shape
    qb = 128
    ck = 512
    ksel = min(TOPK_MAX, slen // 4)
    assert slen % ck == 0 and ck >= ksel
    idx_bits = max(1, int(slen - 1).bit_length())
    body = functools.partial(_dsa_prompt_body, qb=qb, ck=ck, ksel=ksel, idx_bits=idx_bits)
    return pl.pallas_call(
        body,
        grid=(bsz, slen // qb),
        in_specs=[
            pl.BlockSpec((1, IDX_HEADS, qb, IDX_DIM), lambda b, i: (b, 0, i, 0)),
            pl.BlockSpec((1, qb, IDX_HEADS), lambda b, i: (b, i, 0)),
            pl.BlockSpec((1, slen, IDX_DIM), lambda b, i: (b, 0, 0)),
            pl.BlockSpec((1, B_HEADS, qb, HEAD_DIM), lambda b, i: (b, 0, i, 0)),
            pl.BlockSpec((1, N_KV, slen, HEAD_DIM), lambda b, i: (b, 0, 0, 0)),
            pl.BlockSpec((1, N_KV, slen, HEAD_DIM), lambda b, i: (b, 0, 0, 0)),
        ],
        out_specs=pl.BlockSpec((1, qb, B_W), lambda b, i: (b, i, 0)),
        out_shape=jax.ShapeDtypeStruct((bsz, slen, B_W), BF16),
        scratch_shapes=[
            pltpu.VMEM((qb, slen), I32),
            pltpu.VMEM((N_KV, GROUP * qb, 1), F32),
            pltpu.VMEM((N_KV, GROUP * qb, 1), F32),
            pltpu.VMEM((N_KV, GROUP * qb, HEAD_DIM), F32),
        ],
        compiler_params=pltpu.CompilerParams(
            dimension_semantics=("arbitrary", "arbitrary"), vmem_limit_bytes=VMEM_LIMIT),
        name="dsa_prompt",
    )(qih, wi, kib, qh, kg, vg)


def _dsa_dec_body(pt_ref, qi_ref, w_ref, q_ref, kin_ref, kn_ref, vn_ref, cki_hbm, ck_hbm, cv_hbm,
                  o_ref, kibuf, kbuf, vbuf, sems, sk_ref, *, layer, n_pages, t_dec, ksel, idx_bits):
    b = pl.program_id(0)
    past = n_pages * PAGE_SIZE
    tpad = SUBLANES
    total = past + LANES

    def page_copies(p):
        phys = pt_ref[b, p]
        return (pltpu.make_async_copy(cki_hbm.at[layer, phys], kibuf.at[p], sems.at[0, p]),
                pltpu.make_async_copy(ck_hbm.at[layer, phys], kbuf.at[p], sems.at[1, p]),
                pltpu.make_async_copy(cv_hbm.at[layer, phys], vbuf.at[p], sems.at[2, p]))

    def start_page(p, carry):
        for cp in page_copies(p):
            cp.start()
        return carry

    def wait_page(p, carry):
        for cp in page_copies(p):
            cp.wait()
        return carry

    lax.fori_loop(0, n_pages, start_page, 0)
    lax.fori_loop(0, n_pages, wait_page, 0)

    rows1 = (tpad, 1)
    qi = qi_ref[0]
    w = w_ref[0]

    def scores(keys):
        d = lax.dot_general(qi, keys, _NT, preferred_element_type=F32)
        r = jnp.maximum(d, 0.0) * w
        n = r.shape[-1]
        return jnp.sum(r.reshape(IDX_HEADS, tpad, n), axis=0)

    ki_past = kibuf[...].reshape(past, IDX_DIM).astype(BF16)
    sk_ref[:, 0:past] = _sortable_key(scores(ki_past))
    trow = lax.broadcasted_iota(I32, (tpad, LANES), 0)
    tcol = lax.broadcasted_iota(I32, (tpad, LANES), 1)
    sc_new = jnp.where((tcol <= trow) & (tcol < t_dec), scores(kin_ref[0]), -jnp.inf)
    sk_ref[:, past:total] = _sortable_key(sc_new)

    def count_ge(cand):
        return jnp.sum(jnp.where(sk_ref[...] >= cand, 1, 0), axis=1, keepdims=True)

    thr = _kth_largest_key(count_ge, ksel, rows1)
    n_ge = count_ge(thr)
    real_row = lax.broadcasted_iota(I32, rows1, 0) < t_dec
    tie_rows = (n_ge > ksel) & (thr > NEG_INF_KEY) & real_row

    @pl.when(jnp.max(tie_rows.astype(I32)) > 0)
    def _():
        need = ksel - count_ge(thr + 1)
        lane = lax.broadcasted_iota(I32, (tpad, total), 1)

        def count_tied_below(cand):
            return jnp.sum(jnp.where((sk_ref[...] == thr) & (lane < cand), 1, 0), axis=1, keepdims=True)

        cut = _tie_cutoff(count_tied_below, need, idx_bits, rows1)
        blk = sk_ref[...]
        sk_ref[...] = jnp.where((blk == thr) & (lane > cut), blk - 1, blk)

    thr_sel = jnp.maximum(thr, NEG_INF_KEY + 1)
    bias = jnp.where(sk_ref[...] >= thr_sel, 0.0, NEG_BIG)

    k_past = kbuf[...].reshape(past, KV_W).astype(BF16)
    v_past = vbuf[...].reshape(past, KV_W).astype(BF16)
    for g in range(N_KV):
        qg = q_ref[0, g]
        s = jnp.concatenate([
            lax.dot_general(qg, k_past, _NT, preferred_element_type=F32),
            lax.dot_general(qg, kn_ref[0], _NT, preferred_element_type=F32)], axis=1)
        s = (s.reshape(GROUP, tpad, total) + bias[None]).reshape(GROUP * tpad, total)
        p = jnp.exp(s - jnp.max(s, axis=1, keepdims=True))
        l = jnp.sum(p, axis=1, keepdims=True)
        pb = p.astype(BF16)
        o = (jnp.dot(pb[:, 0:past], v_past, preferred_element_type=F32)
             + jnp.dot(pb[:, past:total], vn_ref[0], preferred_element_type=F32))
        o_ref[0, g] = (o / l)[:, g * HEAD_DIM:(g + 1) * HEAD_DIM]


def _dsa_dec_call(page_table, qi_l, w_l, q_l, ki_new, k_new, v_new, cache_kidx, cache_k, cache_v, *, layer, t_dec):
    bd, n_pages = page_table.shape
    past = n_pages * PAGE_SIZE
    total = past + LANES
    ksel = min(TOPK_MAX, (past + t_dec) // 4)
    idx_bits = int(total - 1).bit_length()
    tpad = SUBLANES
    body = functools.partial(_dsa_dec_body, layer=layer, n_pages=n_pages, t_dec=t_dec, ksel=ksel, idx_bits=idx_bits)
    any_spec = pl.BlockSpec(memory_space=pl.ANY)
    grid_spec = pltpu.PrefetchScalarGridSpec(
        num_scalar_prefetch=1,
        grid=(bd,),
        in_specs=[
            pl.BlockSpec((1, IDX_HEADS * tpad, IDX_DIM), lambda b, pt: (b, 0, 0)),
            pl.BlockSpec((1, IDX_HEADS * tpad, 1), lambda b, pt: (b, 0, 0)),
            pl.BlockSpec((1, N_KV, GROUP * tpad, KV_W), lambda b, pt: (b, 0, 0, 0)),
            pl.BlockSpec((1, LANES, IDX_DIM), lambda b, pt: (b, 0, 0)),
            pl.BlockSpec((1, LANES, KV_W), lambda b, pt: (b, 0, 0)),
            pl.BlockSpec((1, LANES, KV_W), lambda b, pt: (b, 0, 0)),
            any_spec, any_spec, any_spec,
        ],
        out_specs=pl.BlockSpec((1, N_KV, GROUP * tpad, HEAD_DIM), lambda b, pt: (b, 0, 0, 0)),
        scratch_shapes=[
            pltpu.VMEM((n_pages, PAGE_SIZE, IDX_DIM), F32),
            pltpu.VMEM((n_pages, PAGE_SIZE, KV_W), F32),
            pltpu.VMEM((n_pages, PAGE_SIZE, KV_W), F32),
            pltpu.SemaphoreType.DMA((3, n_pages)),
            pltpu.VMEM((tpad, total), I32),
        ],
    )
    return pl.pallas_call(
        body,
        grid_spec=grid_spec,
        out_shape=jax.ShapeDtypeStruct((bd, N_KV, GROUP * tpad, HEAD_DIM), F32),
        compiler_params=pltpu.CompilerParams(dimension_semantics=("arbitrary",), vmem_limit_bytes=VMEM_LIMIT),
        name="dsa_dec",
    )(page_table, qi_l, w_l, q_l, ki_new, k_new, v_new, cache_kidx, cache_k, cache_v)


def _dsa_dec(qih, wi, qh, kf, vf, kib, page_table, cache_kidx, cache_k2, cache_v2, *, layer, bd, td):
    tpad = SUBLANES
    md = bd * td

    def head_rows(xh, nh):
        xh = xh[0].reshape(nh, bd, td, HEAD_DIM).transpose(1, 0, 2, 3)
        return jnp.pad(xh, ((0, 0), (0, 0), (0, tpad - td), (0, 0)))

    qi_l = head_rows(qih, IDX_HEADS).reshape(bd, IDX_HEADS * tpad, IDX_DIM)
    w_l = jnp.pad(wi[0].reshape(bd, td, IDX_HEADS).transpose(0, 2, 1), ((0, 0), (0, 0), (0, tpad - td)))
    w_l = w_l.reshape(bd, IDX_HEADS * tpad, 1)
    q_g = head_rows(qh, B_HEADS).reshape(bd, N_KV, GROUP * tpad, HEAD_DIM)
    zeros_q = jnp.zeros_like(q_g)
    q_l = jnp.stack([jnp.concatenate([q_g[:, 0], zeros_q[:, 0]], axis=-1),
                     jnp.concatenate([zeros_q[:, 1], q_g[:, 1]], axis=-1)], axis=1)
    pad_rows = ((0, 0), (0, LANES - td), (0, 0))
    ki_new = jnp.pad(kib[0].reshape(bd, td, IDX_DIM), pad_rows)
    k_new = jnp.pad(kf[0].astype(BF16).reshape(bd, td, KV_W), pad_rows)
    v_new = jnp.pad(vf[0].astype(BF16).reshape(bd, td, KV_W), pad_rows)
    o_l = _dsa_dec_call(page_table, qi_l, w_l, q_l, ki_new, k_new, v_new, cache_kidx, cache_k2, cache_v2,
                        layer=layer, t_dec=td)
    o_g = o_l.reshape(bd, N_KV, GROUP, tpad, HEAD_DIM)[:, :, :, :td]
    return o_g.transpose(0, 3, 1, 2, 4).reshape(1, md, B_W).astype(BF16)


FF_TILE = 256


def _post_body(*refs, m, alpha, t_dec):
    if t_dec:
        (x_ref, a_ref, b_ref, c_ref, wo_ref, g1_ref, b1_ref, wup_ref, cf_ref, wdn_ref, g2_ref, b2_ref,
         p1_ref, p2_ref, y_ref, st_ref, hist_ref) = refs
    else:
        (x_ref, a_ref, b_ref, c_ref, wo_ref, g1_ref, b1_ref, wup_ref, cf_ref, wdn_ref, g2_ref, b2_ref,
         y_ref, st_ref, hist_ref) = refs

    @pl.when(pl.program_id(1) == 0)
    def _():
        hist_ref[0:SUBLANES, :] = jnp.zeros((SUBLANES, 2 * D_FF), F32)

    mix = (jnp.dot(a_ref[0], wo_ref[0:A_W, :], preferred_element_type=F32)
           + jnp.dot(b_ref[0], wo_ref[A_W:A_W + B_W, :], preferred_element_type=F32)
           + jnp.dot(c_ref[0], wo_ref[A_W + B_W:, :], preferred_element_type=F32))
    x1 = _layer_norm(alpha * x_ref[0] + mix, g1_ref[...], b1_ref[...])
    x1b = x1.astype(BF16)

    t = lax.broadcasted_iota(I32, (m, 1), 0) % t_dec if t_dec else None
    f = jnp.zeros((m, D_MODEL), F32)
    for j in range(D_FF // FF_TILE):
        gcols = slice(j * FF_TILE, (j + 1) * FF_TILE)
        ucols = slice(D_FF + j * FF_TILE, D_FF + (j + 1) * FF_TILE)
        halves = []
        for cols in (gcols, ucols):
            hu = jnp.dot(x1b, wup_ref[:, cols], preferred_element_type=F32)
            hist_ref[pl.ds(SUBLANES, m), cols] = hu
            state = (t, p1_ref[:, cols], p2_ref[:, cols]) if t_dec else None
            halves.append(_causal_conv3(hu, hist_ref, cols, cf_ref, m, state))
        gate, up = halves
        act = gate * (1.0 / (1.0 + jnp.exp(-gate))) * up
        f = f + jnp.dot(act.astype(BF16), wdn_ref[gcols, :], preferred_element_type=F32)
    y_ref[0] = _layer_norm(alpha * x1 + f, g2_ref[...], b2_ref[...])

    if t_dec:
        st_ref[0] = hist_ref[pl.ds(SUBLANES, m), :]
    else:
        last = hist_ref[pl.ds(SUBLANES + m - 2, 2), :]
        st_ref[0] = last
        hist_ref[SUBLANES - 2:SUBLANES, :] = last


def _post_call(x, a, bo, c, wo, g1, b1, wup, cf, wdn, g2, b2, p1=None, p2=None, *, tm, alpha, t_dec=0):
    bsz, tlen, _ = x.shape
    nt = tlen // tm

    def row_blk(width):
        return pl.BlockSpec((1, tm, width), lambda b, j: (b, j, 0))

    in_specs = [row_blk(D_MODEL), row_blk(A_W), row_blk(B_W), row_blk(C_W)] + [_vmem_full()] * 8
    args = [x, a, bo, c, wo, g1, b1, wup, cf, wdn, g2, b2]
    if t_dec:
        in_specs += [_vmem_full(), _vmem_full()]
        args += [p1, p2]
        st_shape = jax.ShapeDtypeStruct((bsz, tlen, 2 * D_FF), F32)
        st_spec = row_blk(2 * D_FF)
    else:
        st_shape = jax.ShapeDtypeStruct((bsz, CONV_W - 1, 2 * D_FF), F32)
        st_spec = pl.BlockSpec((1, CONV_W - 1, 2 * D_FF), lambda b, j: (b, 0, 0))
    return pl.pallas_call(
        functools.partial(_post_body, m=tm, alpha=alpha, t_dec=t_dec),
        grid=(bsz, nt),
        in_specs=in_specs,
        out_specs=[row_blk(D_MODEL), st_spec],
        out_shape=[jax.ShapeDtypeStruct((bsz, tlen, D_MODEL), F32), st_shape],
        scratch_shapes=[pltpu.VMEM((SUBLANES + tm, 2 * D_FF), F32)],
        compiler_params=pltpu.CompilerParams(
            dimension_semantics=("arbitrary", "arbitrary"), vmem_limit_bytes=VMEM_LIMIT),
        name="post_dec" if t_dec else "post",
    )(*args)


def _pad_cols(w, width):
    return jnp.pad(w, ((0, 0), (0, width - w.shape[1])))


def _rope_tables(pos):
    inv = ROPE_THETA ** (-jnp.arange(ROT_HALF, dtype=F32) / ROT_HALF)
    ang = pos.astype(F32)[:, None] * inv[None, :]
    cos, sin = jnp.cos(ang), jnp.sin(ang)
    n = pos.shape[0]
    rest = HEAD_DIM - ROT_DIM
    cos_h = jnp.concatenate([cos, cos, jnp.ones((n, rest), F32)], axis=1)
    sin_h = jnp.concatenate([-sin, sin, jnp.zeros((n, rest), F32)], axis=1)
    reps = LANES // HEAD_DIM
    return jnp.tile(cos_h, (1, reps)), jnp.tile(sin_h, (1, reps))


def kernel(x_prompt, x_sample, cache_k, cache_v, cache_kidx, page_table, state_conv, state_ffn_conv, w_in, conv_a,
           ln_cv_g, ln_cv_b, w_sp, b_sp, w_o, ln1_g, ln1_b, w_up, conv_f, w_down, ln2_g, ln2_b):
    depth = w_in.shape[0]
    alpha = (2 * depth) ** 0.25
    bp, sp, _ = x_prompt.shape
    bd, td, _ = x_sample.shape
    n_pages = page_table.shape[1]
    past = n_pages * PAGE_SIZE
    md = bd * td
    tpad = SUBLANES
    assert sp % 512 == 0 and md % CHUNK == 0 and CHUNK % td == 0 and td <= tpad and td <= CHUNK
    tm_proj = 512
    tm_post = 256

    cos_p, sin_p = _rope_tables(jnp.arange(sp, dtype=I32))
    cos_s, sin_s = _rope_tables(jnp.tile(past + jnp.arange(td, dtype=I32), bd))

    n_pool = cache_k.shape[1]
    cache_k2 = cache_k.reshape(depth, n_pool, PAGE_SIZE, KV_W)
    cache_v2 = cache_v.reshape(depth, n_pool, PAGE_SIZE, KV_W)

    tril = jnp.tril(jnp.ones((CHUNK, CHUNK), bool))
    eye_b = jnp.eye(CHUNK // td, dtype=F32)
    t_of_row = jnp.arange(md) % td

    hp, hs = x_prompt, x_sample.reshape(1, md, D_MODEL)
    outs = [[] for _ in range(11)]
    for l in range(depth):
        w = w_in[l]
        w_pad = jnp.concatenate([
            w[:, :OFF_KI],
            _pad_cols(w[:, OFF_KI:OFF_KI + IDX_DIM], LANES),
            _pad_cols(w[:, OFF_KI + IDX_DIM:OFF_KI + IDX_DIM + IDX_HEADS], LANES),
            w[:, OFF_KI + IDX_DIM + IDX_HEADS:]], axis=1).astype(BF16)
        lng, lnb = ln_cv_g[l][None], ln_cv_b[l][None]
        wo_b, wup_b, wdn_b = w_o[l].astype(BF16), w_up[l].astype(BF16), w_down[l].astype(BF16)
        g1, b1, g2, b2 = ln1_g[l][None], ln1_b[l][None], ln2_g[l][None], ln2_b[l][None]

        wm_p = jnp.where(tril[None], w_sp[l], 0).astype(BF16)
        bsr_p = jnp.repeat(b_sp[l].T, HEAD_DIM, axis=1)
        (a_o, c_o, qh, qih, wi, kf, vf, kif, kg, vg, kib, conv_new) = _proj_call(
            hp, w_pad, cos_p, sin_p, conv_a[l], lng, lnb, wm_p, bsr_p, tm=tm_proj)
        b_o = _dsa_prompt_call(qih, wi, kib, qh, kg, vg)
        hp, ffn_new = _post_call(hp, a_o, b_o, c_o, wo_b, g1, b1, wup_b, conv_f[l], wdn_b, g2, b2,
                                 tm=tm_post, alpha=alpha)
        outs[0].append(kf.reshape(bp, sp, N_KV, HEAD_DIM))
        outs[2].append(vf.reshape(bp, sp, N_KV, HEAD_DIM))
        outs[4].append(kif)
        outs[6].append(conv_new)
        outs[8].append(ffn_new)

        ws_small = jnp.where(tril[None, :td, :td], w_sp[l][:, :td, :td], 0)
        wm_s = jnp.einsum("ab,hts->hatbs", eye_b, ws_small).reshape(C_HEADS, CHUNK, CHUNK).astype(BF16)
        bsr_s = jnp.repeat(b_sp[l][:, :td].T, HEAD_DIM, axis=1)[jnp.arange(CHUNK) % td]
        st = state_conv[l]
        p1 = jnp.repeat(st[:, 1], td, axis=0)
        p2 = jnp.where((t_of_row == 0)[:, None], jnp.repeat(st[:, 0], td, axis=0), jnp.repeat(st[:, 1], td, axis=0))
        (a_o, c_o, qh, qih, wi, kf, vf, kif, kg, vg, kib, z_all, vn_all) = _proj_call(
            hs, w_pad, cos_s, sin_s, conv_a[l], lng, lnb, wm_s, bsr_s, p1, p2, tm=md, t_dec=td)

        b_o = _dsa_dec(qih, wi, qh, kf, vf, kib, page_table, cache_kidx, cache_k2, cache_v2, layer=l, bd=bd, td=td)

        fs = state_ffn_conv[l]
        f1 = jnp.repeat(fs[:, 1], td, axis=0)
        f2 = jnp.where((t_of_row == 0)[:, None], jnp.repeat(fs[:, 0], td, axis=0), jnp.repeat(fs[:, 1], td, axis=0))
        hs, ffn_all = _post_call(hs, a_o, b_o, c_o, wo_b, g1, b1, wup_b, conv_f[l], wdn_b, g2, b2, f1, f2,
                                 tm=md, alpha=alpha, t_dec=td)
        outs[1].append(kf.reshape(bd, td, N_KV, HEAD_DIM))
        outs[3].append(vf.reshape(bd, td, N_KV, HEAD_DIM))
        outs[5].append(kif.reshape(bd, td, IDX_DIM))
        outs[7].append(z_all.reshape(bd, td, A_W)[:, td - (CONV_W - 1):])
        outs[9].append(ffn_all.reshape(bd, td, 2 * D_FF)[:, td - (CONV_W - 1):])
        outs[10].append(vn_all.reshape(bd, td, C_W))

    return (hp, hs.reshape(bd, td, D_MODEL)) + tuple(jnp.stack(o) for o in outs)
```

```python
import functools

import jax
import jax.numpy as jnp
import numpy as np
from jax import lax
from jax.experimental import pallas as pl
from jax.experimental.pallas import tpu as pltpu

F32 = jnp.float32
BF16 = jnp.bfloat16
I32 = jnp.int32

D_MODEL = 1024
HEAD_DIM = 64
A_W = D_MODEL // 4
B_W = D_MODEL // 2
C_W = D_MODEL // 4
B_HEADS = B_W // HEAD_DIM
C_HEADS = C_W // HEAD_DIM
N_KV = 2
KV_W = N_KV * HEAD_DIM
GROUP = B_HEADS // N_KV
IDX_HEADS = 8
IDX_DIM = 64
TOPK_MAX = 256
ROT_DIM = HEAD_DIM // 4
ROT_HALF = ROT_DIM // 2
ROPE_THETA = 500000.0
CONV_W = 3
CHUNK = 128
D_FF = 2816
PAGE_SIZE = 128
LN_EPS = 1e-5

LANES = 128
SUBLANES = 8
VMEM_LIMIT = 56 * 1024 * 1024

OFF_GB = 0
OFF_GC = OFF_GB + A_W
OFF_HA = OFF_GC + A_W
OFF_Q = OFF_HA + A_W
OFF_K = OFF_Q + B_W
OFF_V = OFF_K + KV_W
OFF_QI = OFF_V + KV_W
OFF_KI = OFF_QI + IDX_HEADS * IDX_DIM
OFF_WI = OFF_KI + LANES
OFF_U = OFF_WI + LANES
OFF_VC = OFF_U + C_W
N_IN_PAD = OFF_VC + C_W

NEG_BIG = -0.7 * float(np.finfo(np.float32).max)
INT_MIN = -(2 ** 31)
NEG_INF_KEY = int(np.int32(np.uint32(0xFF800000) ^ np.uint32(0x7FFFFFFF)))

_NT = (((1,), (1,)), ((), ()))


def _vmem_full():
    return pl.BlockSpec(memory_space=pltpu.VMEM)


def _layer_norm(x, g, b):
    mu = jnp.mean(x, axis=-1, keepdims=True)
    xc = x - mu
    var = jnp.mean(xc * xc, axis=-1, keepdims=True)
    return xc * lax.rsqrt(var + LN_EPS) * g + b


def _sortable_key(x):
    k = lax.bitcast_convert_type(x, I32)
    return k ^ (lax.shift_right_arithmetic(k, 31) & 0x7FFFFFFF)


def _rotary(seg, cos, sin, first_half):
    n = seg.shape[-1]
    fwd = pltpu.roll(seg, n - ROT_HALF, 1)
    bwd = pltpu.roll(seg, ROT_HALF, 1)
    return seg * cos + jnp.where(first_half, fwd, bwd) * sin


def _causal_conv3(cur, hist_ref, cols, w_ref, m, state=None):
    z1 = hist_ref[pl.ds(SUBLANES - 1, m), cols]
    z2 = hist_ref[pl.ds(SUBLANES - 2, m), cols]
    if state is not None:
        t, p1, p2 = state
        z1 = jnp.where(t >= 1, z1, p1)
        z2 = jnp.where(t >= 2, z2, p2)
    return z2 * w_ref[0:1, cols] + z1 * w_ref[1:2, cols] + cur * w_ref[2:3, cols]


def _proj_body(*refs, m, stateful):
    if stateful:
        (x_ref, w_ref, cos_ref, sin_ref, conva_ref, lng_ref, lnb_ref, wm_ref, bsr_ref, p1_ref, p2_ref,
         a_ref, c_ref, qh_ref, qih_ref, wi_ref, kf_ref, vf_ref, kif_ref, kg_ref, vg_ref, kib_ref, z_ref, vn_ref,
         hist_ref) = refs
    else:
        (x_ref, w_ref, cos_ref, sin_ref, conva_ref, lng_ref, lnb_ref, wm_ref, bsr_ref,
         a_ref, c_ref, qh_ref, qih_ref, wi_ref, kf_ref, vf_ref, kif_ref, kg_ref, vg_ref, kib_ref, z_ref,
         hist_ref) = refs

    first_tile = pl.program_id(1) == 0

    @pl.when(first_tile)
    def _():
        hist_ref[0:SUBLANES, :] = jnp.zeros((SUBLANES, A_W), F32)

    x = x_ref[0].astype(BF16)
    h = jnp.dot(x, w_ref[...], preferred_element_type=F32)

    z = h[:, OFF_GC:OFF_GC + A_W] * h[:, OFF_HA:OFF_HA + A_W]
    hist_ref[pl.ds(SUBLANES, m), :] = z
    state = None
    if stateful:
        t = lax.broadcasted_iota(I32, (m, 1), 0) % stateful
        state = (t, p1_ref[...], p2_ref[...])
    y = _causal_conv3(z, hist_ref, slice(None), conva_ref, m, state)
    a_ref[0] = (h[:, OFF_GB:OFF_GB + A_W] * y).astype(BF16)
    if stateful:
        z_ref[0] = z
    else:
        z_ref[0] = z[m - 2:m, :]
        hist_ref[SUBLANES - 2:SUBLANES, :] = z[m - 2:m, :]

    vn = _layer_norm(h[:, OFF_VC:OFF_VC + C_W], lng_ref[...], lnb_ref[...])
    if stateful:
        vn_ref[0] = vn
    vnb = vn.astype(BF16)
    lane_head = lax.broadcasted_iota(I32, (CHUNK, C_W), 1) // HEAD_DIM
    for r in range(m // CHUNK):
        rows = slice(r * CHUNK, (r + 1) * CHUNK)
        mixed = jnp.zeros((CHUNK, C_W), F32)
        for hh in range(C_HEADS):
            part = jnp.dot(wm_ref[hh], vnb[rows], preferred_element_type=F32)
            mixed = jnp.where(lane_head == hh, part, mixed)
        c_ref[0, rows, :] = (h[rows, OFF_U:OFF_U + C_W] * (mixed + bsr_ref[...])).astype(BF16)

    cos = cos_ref[...]
    sin = sin_ref[...]
    first_half = (lax.broadcasted_iota(I32, (m, LANES), 1) % HEAD_DIM) < ROT_HALF

    def rot(off):
        return _rotary(h[:, off:off + LANES], cos, sin, first_half)

    for j in range(B_W // LANES):
        qr = (rot(OFF_Q + j * LANES) * (HEAD_DIM ** -0.5)).astype(BF16)
        qir = (rot(OFF_QI + j * LANES) * (IDX_DIM ** -0.5)).astype(BF16)
        for e in range(LANES // HEAD_DIM):
            hh = j * (LANES // HEAD_DIM) + e
            qh_ref[0, hh] = qr[:, e * HEAD_DIM:(e + 1) * HEAD_DIM]
            qih_ref[0, hh] = qir[:, e * HEAD_DIM:(e + 1) * HEAD_DIM]
    kr = rot(OFF_K)
    v = h[:, OFF_V:OFF_V + KV_W]
    kf_ref[0] = kr
    vf_ref[0] = v
    vt = v.T
    for g in range(N_KV):
        kg_ref[0, g] = kr[:, g * HEAD_DIM:(g + 1) * HEAD_DIM].astype(BF16)
        vg_ref[0, g] = vt[g * HEAD_DIM:(g + 1) * HEAD_DIM, :].astype(BF16)
    kir = rot(OFF_KI)[:, :IDX_DIM]
    kif_ref[0] = kir
    kib_ref[0] = kir.astype(BF16)
    wi_ref[0] = (h[:, OFF_WI:OFF_WI + LANES] * (IDX_HEADS ** -0.5)).T[0:IDX_HEADS, :]


def _proj_call(x, w, cos, sin, conva, lng, lnb, wm, bsr, p1=None, p2=None, *, tm, t_dec=0):
    bsz, tlen, _ = x.shape
    stateful = t_dec
    nt = tlen // tm
    grid = (bsz, nt)

    def row_blk(width):
        return pl.BlockSpec((1, tm, width), lambda b, j: (b, j, 0))

    def head_blk(nh):
        return pl.BlockSpec((1, nh, tm, HEAD_DIM), lambda b, j: (b, 0, j, 0))

    tab = pl.BlockSpec((tm, LANES), lambda b, j: (j, 0))
    in_specs = [row_blk(D_MODEL), _vmem_full(), tab, tab] + [_vmem_full()] * 5
    args = [x, w, cos, sin, conva, lng, lnb, wm, bsr]
    if stateful:
        in_specs += [_vmem_full(), _vmem_full()]
        args += [p1, p2]
    out_shape = [
        jax.ShapeDtypeStruct((bsz, tlen, A_W), BF16),
        jax.ShapeDtypeStruct((bsz, tlen, C_W), BF16),
        jax.ShapeDtypeStruct((bsz, B_HEADS, tlen, HEAD_DIM), BF16),
        jax.ShapeDtypeStruct((bsz, IDX_HEADS, tlen, IDX_DIM), BF16),
        jax.ShapeDtypeStruct((bsz, IDX_HEADS, tlen), F32),
        jax.ShapeDtypeStruct((bsz, tlen, KV_W), F32),
        jax.ShapeDtypeStruct((bsz, tlen, KV_W), F32),
        jax.ShapeDtypeStruct((bsz, tlen, IDX_DIM), F32),
        jax.ShapeDtypeStruct((bsz, N_KV, tlen, HEAD_DIM), BF16),
        jax.ShapeDtypeStruct((bsz, N_KV, HEAD_DIM, tlen), BF16),
        jax.ShapeDtypeStruct((bsz, tlen, IDX_DIM), BF16),
    ]
    out_specs = [row_blk(A_W), row_blk(C_W), head_blk(B_HEADS), head_blk(IDX_HEADS),
                 pl.BlockSpec((1, IDX_HEADS, tm), lambda b, j: (b, 0, j)),
                 row_blk(KV_W), row_blk(KV_W), row_blk(IDX_DIM), head_blk(N_KV),
                 pl.BlockSpec((1, N_KV, HEAD_DIM, tm), lambda b, j: (b, 0, 0, j)), row_blk(IDX_DIM)]
    if stateful:
        out_shape += [jax.ShapeDtypeStruct((bsz, tlen, A_W), F32), jax.ShapeDtypeStruct((bsz, tlen, C_W), F32)]
        out_specs += [row_blk(A_W), row_blk(C_W)]
    else:
        out_shape += [jax.ShapeDtypeStruct((bsz, CONV_W - 1, A_W), F32)]
        out_specs += [pl.BlockSpec((1, CONV_W - 1, A_W), lambda b, j: (b, 0, 0))]
    return pl.pallas_call(
        functools.partial(_proj_body, m=tm, stateful=stateful),
        grid=grid,
        in_specs=in_specs,
        out_specs=out_specs,
        out_shape=out_shape,
        scratch_shapes=[pltpu.VMEM((SUBLANES + tm, A_W), F32)],
        compiler_params=pltpu.CompilerParams(
            dimension_semantics=("arbitrary", "arbitrary"), vmem_limit_bytes=VMEM_LIMIT),
        name="proj_dec" if stateful else "proj",
    )(*args)


def _kth_largest_key(count_ge, ksel, shape):
    thr = jnp.where(count_ge(jnp.zeros(shape, I32)) >= ksel, 0, INT_MIN).astype(I32)

    def bit_step(it, thr):
        cand = thr + lax.shift_left(jnp.int32(1), 30 - it)
        return jnp.where(count_ge(cand) >= ksel, cand, thr)

    return lax.fori_loop(0, 31, bit_step, thr)


def _tie_cutoff(count_tied_below, need, nbits, shape):
    def bit_step(it, cut):
        cand = cut + lax.shift_left(jnp.int32(1), nbits - 1 - it)
        return jnp.where(count_tied_below(cand) <= need - 1, cand, cut)

    return lax.fori_loop(0, nbits, bit_step, jnp.zeros(shape, I32))


def _dsa_prompt_body(qi_ref, w_ref, ki_ref, q_ref, k_ref, v_ref, o_ref, sk_ref, m_ref, l_ref, acc_ref,
                     *, qb, ck, ca, ksel, idx_bits):
    i = pl.program_id(1)
    nck = (i * qb) // ck + 1
    cols1 = (1, qb)

    qi_all = qi_ref[0].reshape(IDX_HEADS * qb, IDX_DIM)
    w = w_ref[0]
    diff = lax.broadcasted_iota(I32, (ck, qb), 0) - lax.broadcasted_iota(I32, (ck, qb), 1)

    def score_chunk(c, carry):
        off = pl.multiple_of(c * ck, ck)
        d = lax.dot_general(ki_ref[0, pl.ds(off, ck), :], qi_all, _NT, preferred_element_type=F32)
        sc = jnp.zeros((ck, qb), F32)
        for hh in range(IDX_HEADS):
            sc = sc + jnp.maximum(d[:, hh * qb:(hh + 1) * qb], 0.0) * w[hh:hh + 1, :]
        sc = jnp.where(diff <= i * qb - off, sc, -jnp.inf)
        sk_ref[pl.ds(off, ck), :] = _sortable_key(sc)
        return carry

    lax.fori_loop(0, nck, score_chunk, 0)

    def count_keys(pred):
        def body(c, acc):
            off = pl.multiple_of(c * ck, ck)
            hit = jnp.where(pred(sk_ref[pl.ds(off, ck), :], off), 1, 0)
            return acc + jnp.sum(hit.reshape(ck // SUBLANES, SUBLANES, qb), axis=0)

        acc = lax.fori_loop(0, nck, body, jnp.zeros((SUBLANES, qb), I32))
        return jnp.sum(acc, axis=0, keepdims=True)

    def count_ge(cand):
        return count_keys(lambda blk, off: blk >= cand)

    thr = _kth_largest_key(count_ge, ksel, cols1)

    n_ge = count_ge(thr)
    tie_cols = (n_ge > ksel) & (thr > NEG_INF_KEY)

    @pl.when(jnp.max(tie_cols.astype(I32)) > 0)
    def _():
        need = ksel - count_ge(thr + 1)
        kpos = lax.broadcasted_iota(I32, (ck, qb), 0)

        def count_tied_below(cand):
            return count_keys(lambda blk, off: (blk == thr) & (kpos + off < cand))

        cut = _tie_cutoff(count_tied_below, need, idx_bits, cols1)

        def demote(c, carry):
            off = pl.multiple_of(c * ck, ck)
            blk = sk_ref[pl.ds(off, ck), :]
            sk_ref[pl.ds(off, ck), :] = jnp.where((blk == thr) & (kpos + off > cut), blk - 1, blk)
            return carry

        lax.fori_loop(0, nck, demote, 0)

    thr_sel = jnp.maximum(thr, NEG_INF_KEY + 1)

    m_ref[...] = jnp.full(m_ref.shape, NEG_BIG, F32)
    l_ref[...] = jnp.zeros(l_ref.shape, F32)
    acc_ref[...] = jnp.zeros(acc_ref.shape, F32)

    def attend_chunk(c, carry):
        off = pl.multiple_of(c * ca, ca)
        bias = jnp.where(sk_ref[pl.ds(off, ca), :] >= thr_sel, 0.0, NEG_BIG)
        bias = jnp.concatenate([bias] * GROUP, axis=1)
        scores = []
        for g in range(N_KV):
            qg = q_ref[0, g * GROUP:(g + 1) * GROUP].reshape(GROUP * qb, HEAD_DIM)
            scores.append(lax.dot_general(k_ref[0, g, pl.ds(off, ca), :], qg, _NT, preferred_element_type=F32))
        for g in range(N_KV):
            s = scores[g] + bias
            m_old = m_ref[g]
            m_new = jnp.maximum(m_old, jnp.max(s, axis=0, keepdims=True))
            alpha = jnp.exp(m_old - m_new)
            p = jnp.exp(s - m_new)
            l_ref[g] = alpha * l_ref[g] + jnp.sum(p, axis=0, keepdims=True)
            acc_ref[g] = alpha * acc_ref[g] + jnp.dot(
                v_ref[0, g, :, pl.ds(off, ca)], p.astype(BF16), preferred_element_type=F32)
            m_ref[g] = m_new
        return carry

    lax.fori_loop(0, (i * qb) // ca + 1, attend_chunk, 0)

    for g in range(N_KV):
        o = acc_ref[g] / l_ref[g]
        for r in range(GROUP):
            hh = g * GROUP + r
            o_ref[0, :, hh * HEAD_DIM:(hh + 1) * HEAD_DIM] = o[:, r * qb:(r + 1) * qb].T.astype(o_ref.dtype)


def _dsa_prompt_call(qih, wi, kib, qh, kg, vg):
    bsz, _, slen, _ = qh.shape
    qb = 128
    ck = 512
    ksel = min(TOPK_MAX, slen // 4)
    assert slen % ck == 0 and ck >= ksel
    idx_bits = max(1, int(slen - 1).bit_length())
    ca = 512
    body = functools.partial(_dsa_prompt_body, qb=qb, ck=ck, ca=ca, ksel=ksel, idx_bits=idx_bits)
    return pl.pallas_call(
        body,
        grid=(bsz, slen // qb),
        in_specs=[
            pl.BlockSpec((1, IDX_HEADS, qb, IDX_DIM), lambda b, i: (b, 0, i, 0)),
            pl.BlockSpec((1, IDX_HEADS, qb), lambda b, i: (b, 0, i)),
            pl.BlockSpec((1, slen, IDX_DIM), lambda b, i: (b, 0, 0)),
            pl.BlockSpec((1, B_HEADS, qb, HEAD_DIM), lambda b, i: (b, 0, i, 0)),
            pl.BlockSpec((1, N_KV, slen, HEAD_DIM), lambda b, i: (b, 0, 0, 0)),
            pl.BlockSpec((1, N_KV, HEAD_DIM, slen), lambda b, i: (b, 0, 0, 0)),
        ],
        out_specs=pl.BlockSpec((1, qb, B_W), lambda b, i: (b, i, 0)),
        out_shape=jax.ShapeDtypeStruct((bsz, slen, B_W), BF16),
        scratch_shapes=[
            pltpu.VMEM((slen, qb), I32),
            pltpu.VMEM((N_KV, 1, GROUP * qb), F32),
            pltpu.VMEM((N_KV, 1, GROUP * qb), F32),
            pltpu.VMEM((N_KV, HEAD_DIM, GROUP * qb), F32),
        ],
        compiler_params=pltpu.CompilerParams(
            dimension_semantics=("arbitrary", "arbitrary"), vmem_limit_bytes=VMEM_LIMIT),
        name="dsa_prompt",
    )(qih, wi, kib, qh, kg, vg)


def _dsa_dec_body(pt_ref, qi_ref, w_ref, q_ref, kin_ref, kn_ref, vn_ref, cki_hbm, ck_hbm, cv_hbm,
                  o_ref, kibuf, kbuf, vbuf, sems, sk_ref, *, layer, n_pages, t_dec, ksel, idx_bits):
    b = pl.program_id(0)
    past = n_pages * PAGE_SIZE
    tpad = SUBLANES
    total = past + LANES

    def page_copies(p):
        phys = pt_ref[b, p]
        dst = pl.ds(pl.multiple_of(p * PAGE_SIZE, PAGE_SIZE), PAGE_SIZE)
        return (pltpu.make_async_copy(cki_hbm.at[layer, phys], kibuf.at[:, dst], sems.at[0, p]),
                pltpu.make_async_copy(ck_hbm.at[layer, phys], kbuf.at[:, dst], sems.at[1, p]),
                pltpu.make_async_copy(cv_hbm.at[layer, phys], vbuf.at[:, dst], sems.at[2, p]))

    def start_page(p, carry):
        for cp in page_copies(p):
            cp.start()
        return carry

    def wait_page(p, carry):
        for cp in page_copies(p):
            cp.wait()
        return carry

    lax.fori_loop(0, n_pages, start_page, 0)
    lax.fori_loop(0, n_pages, wait_page, 0)

    rows1 = (tpad, 1)
    qi = qi_ref[0]
    w = w_ref[0]

    def scores(d):
        r = jnp.maximum(d, 0.0) * w
        n = r.shape[-1]
        return jnp.sum(r.reshape(IDX_HEADS, tpad, n), axis=0)

    d_past = jnp.dot(qi, kibuf[...].astype(BF16), preferred_element_type=F32)
    sk_ref[:, 0:past] = _sortable_key(scores(d_past))
    trow = lax.broadcasted_iota(I32, (tpad, LANES), 0)
    tcol = lax.broadcasted_iota(I32, (tpad, LANES), 1)
    d_new = lax.dot_general(qi, kin_ref[0], _NT, preferred_element_type=F32)
    sc_new = jnp.where((tcol <= trow) & (tcol < t_dec), scores(d_new), -jnp.inf)
    sk_ref[:, past:total] = _sortable_key(sc_new)

    def count_ge(cand):
        return jnp.sum(jnp.where(sk_ref[...] >= cand, 1, 0), axis=1, keepdims=True)

    thr = _kth_largest_key(count_ge, ksel, rows1)
    n_ge = count_ge(thr)
    real_row = lax.broadcasted_iota(I32, rows1, 0) < t_dec
    tie_rows = (n_ge > ksel) & (thr > NEG_INF_KEY) & real_row

    @pl.when(jnp.max(tie_rows.astype(I32)) > 0)
    def _():
        need = ksel - count_ge(thr + 1)
        lane = lax.broadcasted_iota(I32, (tpad, total), 1)

        def count_tied_below(cand):
            return jnp.sum(jnp.where((sk_ref[...] == thr) & (lane < cand), 1, 0), axis=1, keepdims=True)

        cut = _tie_cutoff(count_tied_below, need, idx_bits, rows1)
        blk = sk_ref[...]
        sk_ref[...] = jnp.where((blk == thr) & (lane > cut), blk - 1, blk)

    thr_sel = jnp.maximum(thr, NEG_INF_KEY + 1)
    bias = jnp.where(sk_ref[...] >= thr_sel, 0.0, NEG_BIG)

    kt_past = kbuf[...].astype(BF16)
    vt_past = vbuf[...].astype(BF16)
    for g in range(N_KV):
        qg = q_ref[0, g]
        s = jnp.concatenate([
            jnp.dot(qg, kt_past, preferred_element_type=F32),
            lax.dot_general(qg, kn_ref[0], _NT, preferred_element_type=F32)], axis=1)
        s = (s.reshape(GROUP, tpad, total) + bias[None]).reshape(GROUP * tpad, total)
        p = jnp.exp(s - jnp.max(s, axis=1, keepdims=True))
        l = jnp.sum(p, axis=1, keepdims=True)
        pb = p.astype(BF16)
        o = (lax.dot_general(pb[:, 0:past], vt_past, _NT, preferred_element_type=F32)
             + jnp.dot(pb[:, past:total], vn_ref[0], preferred_element_type=F32))
        o_ref[0, g] = (o / l)[:, g * HEAD_DIM:(g + 1) * HEAD_DIM]


def _dsa_dec_call(page_table, qi_l, w_l, q_l, ki_new, k_new, v_new, cache_kidx, cache_k, cache_v, *, layer, t_dec):
    bd, n_pages = page_table.shape
    past = n_pages * PAGE_SIZE
    total = past + LANES
    ksel = min(TOPK_MAX, (past + t_dec) // 4)
    idx_bits = int(total - 1).bit_length()
    tpad = SUBLANES
    body = functools.partial(_dsa_dec_body, layer=layer, n_pages=n_pages, t_dec=t_dec, ksel=ksel, idx_bits=idx_bits)
    any_spec = pl.BlockSpec(memory_space=pl.ANY)
    grid_spec = pltpu.PrefetchScalarGridSpec(
        num_scalar_prefetch=1,
        grid=(bd,),
        in_specs=[
            pl.BlockSpec((1, IDX_HEADS * tpad, IDX_DIM), lambda b, pt: (b, 0, 0)),
            pl.BlockSpec((1, IDX_HEADS * tpad, 1), lambda b, pt: (b, 0, 0)),
            pl.BlockSpec((1, N_KV, GROUP * tpad, KV_W), lambda b, pt: (b, 0, 0, 0)),
            pl.BlockSpec((1, LANES, IDX_DIM), lambda b, pt: (b, 0, 0)),
            pl.BlockSpec((1, LANES, KV_W), lambda b, pt: (b, 0, 0)),
            pl.BlockSpec((1, LANES, KV_W), lambda b, pt: (b, 0, 0)),
            any_spec, any_spec, any_spec,
        ],
        out_specs=pl.BlockSpec((1, N_KV, GROUP * tpad, HEAD_DIM), lambda b, pt: (b, 0, 0, 0)),
        scratch_shapes=[
            pltpu.VMEM((IDX_DIM, past), F32),
            pltpu.VMEM((KV_W, past), F32),
            pltpu.VMEM((KV_W, past), F32),
            pltpu.SemaphoreType.DMA((3, n_pages)),
            pltpu.VMEM((tpad, total), I32),
        ],
    )
    return pl.pallas_call(
        body,
        grid_spec=grid_spec,
        out_shape=jax.ShapeDtypeStruct((bd, N_KV, GROUP * tpad, HEAD_DIM), F32),
        compiler_params=pltpu.CompilerParams(dimension_semantics=("arbitrary",), vmem_limit_bytes=VMEM_LIMIT),
        name="dsa_dec",
    )(page_table, qi_l, w_l, q_l, ki_new, k_new, v_new, cache_kidx, cache_k, cache_v)


def _dsa_dec(qih, wi, qh, kf, vf, kib, page_table, cache_kidx, cache_k2, cache_v2, *, layer, bd, td):
    tpad = SUBLANES
    md = bd * td

    def head_rows(xh, nh):
        xh = xh[0].reshape(nh, bd, td, HEAD_DIM).transpose(1, 0, 2, 3)
        return jnp.pad(xh, ((0, 0), (0, 0), (0, tpad - td), (0, 0)))

    qi_l = head_rows(qih, IDX_HEADS).reshape(bd, IDX_HEADS * tpad, IDX_DIM)
    w_l = jnp.pad(wi[0].reshape(IDX_HEADS, bd, td).transpose(1, 0, 2), ((0, 0), (0, 0), (0, tpad - td)))
    w_l = w_l.reshape(bd, IDX_HEADS * tpad, 1)
    q_g = head_rows(qh, B_HEADS).reshape(bd, N_KV, GROUP * tpad, HEAD_DIM)
    zeros_q = jnp.zeros_like(q_g)
    q_l = jnp.stack([jnp.concatenate([q_g[:, 0], zeros_q[:, 0]], axis=-1),
                     jnp.concatenate([zeros_q[:, 1], q_g[:, 1]], axis=-1)], axis=1)
    pad_rows = ((0, 0), (0, LANES - td), (0, 0))
    ki_new = jnp.pad(kib[0].reshape(bd, td, IDX_DIM), pad_rows)
    k_new = jnp.pad(kf[0].astype(BF16).reshape(bd, td, KV_W), pad_rows)
    v_new = jnp.pad(vf[0].astype(BF16).reshape(bd, td, KV_W), pad_rows)
    o_l = _dsa_dec_call(page_table, qi_l, w_l, q_l, ki_new, k_new, v_new, cache_kidx, cache_k2, cache_v2,
                        layer=layer, t_dec=td)
    o_g = o_l.reshape(bd, N_KV, GROUP, tpad, HEAD_DIM)[:, :, :, :td]
    return o_g.transpose(0, 3, 1, 2, 4).reshape(1, md, B_W).astype(BF16)


FF_TILE = 256


def _post_body(*refs, m, alpha, t_dec):
    if t_dec:
        (x_ref, a_ref, b_ref, c_ref, wo_ref, g1_ref, b1_ref, wup_ref, cf_ref, wdn_ref, g2_ref, b2_ref,
         p1_ref, p2_ref, y_ref, st_ref, hist_ref) = refs
    else:
        (x_ref, a_ref, b_ref, c_ref, wo_ref, g1_ref, b1_ref, wup_ref, cf_ref, wdn_ref, g2_ref, b2_ref,
         y_ref, st_ref, hist_ref) = refs

    @pl.when(pl.program_id(1) == 0)
    def _():
        hist_ref[0:SUBLANES, :] = jnp.zeros((SUBLANES, 2 * D_FF), F32)

    mix = (jnp.dot(a_ref[0], wo_ref[0:A_W, :], preferred_element_type=F32)
           + jnp.dot(b_ref[0], wo_ref[A_W:A_W + B_W, :], preferred_element_type=F32)
           + jnp.dot(c_ref[0], wo_ref[A_W + B_W:, :], preferred_element_type=F32))
    x1 = _layer_norm(alpha * x_ref[0] + mix, g1_ref[...], b1_ref[...])
    x1b = x1.astype(BF16)

    t = lax.broadcasted_iota(I32, (m, 1), 0) % t_dec if t_dec else None
    f = jnp.zeros((m, D_MODEL), F32)
    for j in range(D_FF // FF_TILE):
        gcols = slice(j * FF_TILE, (j + 1) * FF_TILE)
        ucols = slice(D_FF + j * FF_TILE, D_FF + (j + 1) * FF_TILE)
        halves = []
        for cols in (gcols, ucols):
            hu = jnp.dot(x1b, wup_ref[:, cols], preferred_element_type=F32)
            hist_ref[pl.ds(SUBLANES, m), cols] = hu
            state = (t, p1_ref[:, cols], p2_ref[:, cols]) if t_dec else None
            halves.append(_causal_conv3(hu, hist_ref, cols, cf_ref, m, state))
        gate, up = halves
        act = gate * (1.0 / (1.0 + jnp.exp(-gate))) * up
        f = f + jnp.dot(act.astype(BF16), wdn_ref[gcols, :], preferred_element_type=F32)
    y_ref[0] = _layer_norm(alpha * x1 + f, g2_ref[...], b2_ref[...])

    if t_dec:
        st_ref[0] = hist_ref[pl.ds(SUBLANES, m), :]
    else:
        last = hist_ref[pl.ds(SUBLANES + m - 2, 2), :]
        st_ref[0] = last
        hist_ref[SUBLANES - 2:SUBLANES, :] = last


def _post_call(x, a, bo, c, wo, g1, b1, wup, cf, wdn, g2, b2, p1=None, p2=None, *, tm, alpha, t_dec=0):
    bsz, tlen, _ = x.shape
    nt = tlen // tm

    def row_blk(width):
        return pl.BlockSpec((1, tm, width), lambda b, j: (b, j, 0))

    in_specs = [row_blk(D_MODEL), row_blk(A_W), row_blk(B_W), row_blk(C_W)] + [_vmem_full()] * 8
    args = [x, a, bo, c, wo, g1, b1, wup, cf, wdn, g2, b2]
    if t_dec:
        in_specs += [_vmem_full(), _vmem_full()]
        args += [p1, p2]
        st_shape = jax.ShapeDtypeStruct((bsz, tlen, 2 * D_FF), F32)
        st_spec = row_blk(2 * D_FF)
    else:
        st_shape = jax.ShapeDtypeStruct((bsz, CONV_W - 1, 2 * D_FF), F32)
        st_spec = pl.BlockSpec((1, CONV_W - 1, 2 * D_FF), lambda b, j: (b, 0, 0))
    return pl.pallas_call(
        functools.partial(_post_body, m=tm, alpha=alpha, t_dec=t_dec),
        grid=(bsz, nt),
        in_specs=in_specs,
        out_specs=[row_blk(D_MODEL), st_spec],
        out_shape=[jax.ShapeDtypeStruct((bsz, tlen, D_MODEL), F32), st_shape],
        scratch_shapes=[pltpu.VMEM((SUBLANES + tm, 2 * D_FF), F32)],
        compiler_params=pltpu.CompilerParams(
            dimension_semantics=("arbitrary", "arbitrary"), vmem_limit_bytes=VMEM_LIMIT),
        name="post_dec" if t_dec else "post",
    )(*args)


def _pad_cols(w, width):
    return jnp.pad(w, ((0, 0), (0, width - w.shape[1])))


def _rope_tables(pos):
    inv = ROPE_THETA ** (-jnp.arange(ROT_HALF, dtype=F32) / ROT_HALF)
    ang = pos.astype(F32)[:, None] * inv[None, :]
    cos, sin = jnp.cos(ang), jnp.sin(ang)
    n = pos.shape[0]
    rest = HEAD_DIM - ROT_DIM
    cos_h = jnp.concatenate([cos, cos, jnp.ones((n, rest), F32)], axis=1)
    sin_h = jnp.concatenate([-sin, sin, jnp.zeros((n, rest), F32)], axis=1)
    reps = LANES // HEAD_DIM
    return jnp.tile(cos_h, (1, reps)), jnp.tile(sin_h, (1, reps))


def kernel(x_prompt, x_sample, cache_k, cache_v, cache_kidx, page_table, state_conv, state_ffn_conv, w_in, conv_a,
           ln_cv_g, ln_cv_b, w_sp, b_sp, w_o, ln1_g, ln1_b, w_up, conv_f, w_down, ln2_g, ln2_b):
    depth = w_in.shape[0]
    alpha = (2 * depth) ** 0.25
    bp, sp, _ = x_prompt.shape
    bd, td, _ = x_sample.shape
    n_pages = page_table.shape[1]
    past = n_pages * PAGE_SIZE
    md = bd * td
    tpad = SUBLANES
    assert sp % 512 == 0 and md % CHUNK == 0 and CHUNK % td == 0 and td <= tpad and td <= CHUNK
    tm_proj = 512
    tm_post = 256

    cos_p, sin_p = _rope_tables(jnp.arange(sp, dtype=I32))
    cos_s, sin_s = _rope_tables(jnp.tile(past + jnp.arange(td, dtype=I32), bd))

    n_pool = cache_k.shape[1]
    cache_k2 = cache_k.transpose(0, 1, 3, 4, 2).reshape(depth, n_pool, KV_W, PAGE_SIZE)
    cache_v2 = cache_v.transpose(0, 1, 3, 4, 2).reshape(depth, n_pool, KV_W, PAGE_SIZE)
    cache_kidx = cache_kidx.transpose(0, 1, 3, 2)

    tril = jnp.tril(jnp.ones((CHUNK, CHUNK), bool))
    eye_b = jnp.eye(CHUNK // td, dtype=F32)
    t_of_row = jnp.arange(md) % td

    hp, hs = x_prompt, x_sample.reshape(1, md, D_MODEL)
    outs = [[] for _ in range(11)]
    for l in range(depth):
        w = w_in[l]
        w_pad = jnp.concatenate([
            w[:, :OFF_KI],
            _pad_cols(w[:, OFF_KI:OFF_KI + IDX_DIM], LANES),
            _pad_cols(w[:, OFF_KI + IDX_DIM:OFF_KI + IDX_DIM + IDX_HEADS], LANES),
            w[:, OFF_KI + IDX_DIM + IDX_HEADS:]], axis=1).astype(BF16)
        lng, lnb = ln_cv_g[l][None], ln_cv_b[l][None]
        wo_b, wup_b, wdn_b = w_o[l].astype(BF16), w_up[l].astype(BF16), w_down[l].astype(BF16)
        g1, b1, g2, b2 = ln1_g[l][None], ln1_b[l][None], ln2_g[l][None], ln2_b[l][None]

        wm_p = jnp.where(tril[None], w_sp[l], 0).astype(BF16)
        bsr_p = jnp.repeat(b_sp[l].T, HEAD_DIM, axis=1)
        (a_o, c_o, qh, qih, wi, kf, vf, kif, kg, vg, kib, conv_new) = _proj_call(
            hp, w_pad, cos_p, sin_p, conv_a[l], lng, lnb, wm_p, bsr_p, tm=tm_proj)
        b_o = _dsa_prompt_call(qih, wi, kib, qh, kg, vg)
        hp, ffn_new = _post_call(hp, a_o, b_o, c_o, wo_b, g1, b1, wup_b, conv_f[l], wdn_b, g2, b2,
                                 tm=tm_post, alpha=alpha)
        outs[0].append(kf.reshape(bp, sp, N_KV, HEAD_DIM))
        outs[2].append(vf.reshape(bp, sp, N_KV, HEAD_DIM))
        outs[4].append(kif)
        outs[6].append(conv_new)
        outs[8].append(ffn_new)

        ws_small = jnp.where(tril[None, :td, :td], w_sp[l][:, :td, :td], 0)
        wm_s = jnp.einsum("ab,hts->hatbs", eye_b, ws_small).reshape(C_HEADS, CHUNK, CHUNK).astype(BF16)
        bsr_s = jnp.repeat(b_sp[l][:, :td].T, HEAD_DIM, axis=1)[jnp.arange(CHUNK) % td]
        st = state_conv[l]
        p1 = jnp.repeat(st[:, 1], td, axis=0)
        p2 = jnp.where((t_of_row == 0)[:, None], jnp.repeat(st[:, 0], td, axis=0), jnp.repeat(st[:, 1], td, axis=0))
        (a_o, c_o, qh, qih, wi, kf, vf, kif, kg, vg, kib, z_all, vn_all) = _proj_call(
            hs, w_pad, cos_s, sin_s, conv_a[l], lng, lnb, wm_s, bsr_s, p1, p2, tm=md, t_dec=td)

        b_o = _dsa_dec(qih, wi, qh, kf, vf, kib, page_table, cache_kidx, cache_k2, cache_v2, layer=l, bd=bd, td=td)

        fs = state_ffn_conv[l]
        f1 = jnp.repeat(fs[:, 1], td, axis=0)
        f2 = jnp.where((t_of_row == 0)[:, None], jnp.repeat(fs[:, 0], td, axis=0), jnp.repeat(fs[:, 1], td, axis=0))
        hs, ffn_all = _post_call(hs, a_o, b_o, c_o, wo_b, g1, b1, wup_b, conv_f[l], wdn_b, g2, b2, f1, f2,
                                 tm=md, alpha=alpha, t_dec=td)
        outs[1].append(kf.reshape(bd, td, N_KV, HEAD_DIM))
        outs[3].append(vf.reshape(bd, td, N_KV, HEAD_DIM))
        outs[5].append(kif.reshape(bd, td, IDX_DIM))
        outs[7].append(z_all.reshape(bd, td, A_W)[:, td - (CONV_W - 1):])
        outs[9].append(ffn_all.reshape(bd, td, 2 * D_FF)[:, td - (CONV_W - 1):])
        outs[10].append(vn_all.reshape(bd, td, C_W))

    return (hp, hs.reshape(bd, td, D_MODEL)) + tuple(jnp.stack(o) for o in outs)
```

```python
import functools

import jax
import jax.numpy as jnp
import numpy as np
from jax import lax
from jax.experimental import pallas as pl
from jax.experimental.pallas import tpu as pltpu

F32 = jnp.float32
BF16 = jnp.bfloat16
I32 = jnp.int32

D_MODEL = 1024
HEAD_DIM = 64
A_W = D_MODEL // 4
B_W = D_MODEL // 2
C_W = D_MODEL // 4
B_HEADS = B_W // HEAD_DIM
C_HEADS = C_W // HEAD_DIM
N_KV = 2
KV_W = N_KV * HEAD_DIM
GROUP = B_HEADS // N_KV
IDX_HEADS = 8
IDX_DIM = 64
TOPK_MAX = 256
ROT_DIM = HEAD_DIM // 4
ROT_HALF = ROT_DIM // 2
ROPE_THETA = 500000.0
CONV_W = 3
CHUNK = 128
D_FF = 2816
PAGE_SIZE = 128
LN_EPS = 1e-5

I16 = jnp.int16
LANES = 128
ONES_ROWS = 16
SUBLANES = 8
VMEM_LIMIT = 56 * 1024 * 1024

OFF_GB = 0
OFF_GC = OFF_GB + A_W
OFF_HA = OFF_GC + A_W
OFF_Q = OFF_HA + A_W
OFF_K = OFF_Q + B_W
OFF_V = OFF_K + KV_W
OFF_QI = OFF_V + KV_W
OFF_KI = OFF_QI + IDX_HEADS * IDX_DIM
OFF_WI = OFF_KI + LANES
OFF_U = OFF_WI + LANES
OFF_VC = OFF_U + C_W
N_IN_PAD = OFF_VC + C_W

NEG_BIG = -0.7 * float(np.finfo(np.float32).max)
INT_MIN = -(2 ** 31)
NEG_INF_KEY = int(np.int32(np.uint32(0xFF800000) ^ np.uint32(0x7FFFFFFF)))

_NT = (((1,), (1,)), ((), ()))


def _vmem_full():
    return pl.BlockSpec(memory_space=pltpu.VMEM)


def _layer_norm(x, g, b):
    mu = jnp.mean(x, axis=-1, keepdims=True)
    xc = x - mu
    var = jnp.mean(xc * xc, axis=-1, keepdims=True)
    return xc * lax.rsqrt(var + LN_EPS) * g + b


def _sortable_key(x):
    k = lax.bitcast_convert_type(x, I32)
    return k ^ (lax.shift_right_arithmetic(k, 31) & 0x7FFFFFFF)


def _rotary(seg, cos, sin, first_half):
    n = seg.shape[-1]
    fwd = pltpu.roll(seg, n - ROT_HALF, 1)
    bwd = pltpu.roll(seg, ROT_HALF, 1)
    return seg * cos + jnp.where(first_half, fwd, bwd) * sin


def _causal_conv3(cur, hist_ref, cols, w_ref, m, state=None):
    z1 = hist_ref[pl.ds(SUBLANES - 1, m), cols]
    z2 = hist_ref[pl.ds(SUBLANES - 2, m), cols]
    if state is not None:
        t, p1, p2 = state
        z1 = jnp.where(t >= 1, z1, p1)
        z2 = jnp.where(t >= 2, z2, p2)
    return z2 * w_ref[0:1, cols] + z1 * w_ref[1:2, cols] + cur * w_ref[2:3, cols]


def _proj_body(*refs, m, stateful):
    if stateful:
        (x_ref, w_ref, cos_ref, sin_ref, conva_ref, lng_ref, lnb_ref, wm_ref, bsr_ref, p1_ref, p2_ref,
         a_ref, c_ref, qh_ref, qih_ref, wi_ref, kf_ref, vf_ref, kif_ref, kg_ref, vg_ref, kib_ref, z_ref, vn_ref,
         hist_ref) = refs
    else:
        (x_ref, w_ref, cos_ref, sin_ref, conva_ref, lng_ref, lnb_ref, wm_ref, bsr_ref,
         a_ref, c_ref, qh_ref, qih_ref, wi_ref, kf_ref, vf_ref, kif_ref, kg_ref, vg_ref, kib_ref, z_ref,
         hist_ref) = refs

    first_tile = pl.program_id(1) == 0

    @pl.when(first_tile)
    def _():
        hist_ref[0:SUBLANES, :] = jnp.zeros((SUBLANES, A_W), F32)

    x = x_ref[0].astype(BF16)
    h = jnp.dot(x, w_ref[...], preferred_element_type=F32)

    z = h[:, OFF_GC:OFF_GC + A_W] * h[:, OFF_HA:OFF_HA + A_W]
    hist_ref[pl.ds(SUBLANES, m), :] = z
    state = None
    if stateful:
        t = lax.broadcasted_iota(I32, (m, 1), 0) % stateful
        state = (t, p1_ref[...], p2_ref[...])
    y = _causal_conv3(z, hist_ref, slice(None), conva_ref, m, state)
    a_ref[0] = (h[:, OFF_GB:OFF_GB + A_W] * y).astype(BF16)
    if stateful:
        z_ref[0] = z
    else:
        z_ref[0] = z[m - 2:m, :]
        hist_ref[SUBLANES - 2:SUBLANES, :] = z[m - 2:m, :]

    vn = _layer_norm(h[:, OFF_VC:OFF_VC + C_W], lng_ref[...], lnb_ref[...])
    if stateful:
        vn_ref[0] = vn
    vnb = vn.astype(BF16)
    lane_head = lax.broadcasted_iota(I32, (CHUNK, C_W), 1) // HEAD_DIM
    for r in range(m // CHUNK):
        rows = slice(r * CHUNK, (r + 1) * CHUNK)
        mixed = jnp.zeros((CHUNK, C_W), F32)
        for hh in range(C_HEADS):
            part = jnp.dot(wm_ref[hh], vnb[rows], preferred_element_type=F32)
            mixed = jnp.where(lane_head == hh, part, mixed)
        c_ref[0, rows, :] = (h[rows, OFF_U:OFF_U + C_W] * (mixed + bsr_ref[...])).astype(BF16)

    cos = cos_ref[...]
    sin = sin_ref[...]
    first_half = (lax.broadcasted_iota(I32, (m, LANES), 1) % HEAD_DIM) < ROT_HALF

    def rot(off):
        return _rotary(h[:, off:off + LANES], cos, sin, first_half)

    for j in range(B_W // LANES):
        qr = (rot(OFF_Q + j * LANES) * (HEAD_DIM ** -0.5)).astype(BF16)
        qir = (rot(OFF_QI + j * LANES) * (IDX_DIM ** -0.5)).astype(BF16)
        for e in range(LANES // HEAD_DIM):
            hh = j * (LANES // HEAD_DIM) + e
            qh_ref[0, hh] = qr[:, e * HEAD_DIM:(e + 1) * HEAD_DIM]
            qih_ref[0, hh] = qir[:, e * HEAD_DIM:(e + 1) * HEAD_DIM]
    kr = rot(OFF_K)
    v = h[:, OFF_V:OFF_V + KV_W]
    kf_ref[0] = kr
    vf_ref[0] = v
    kg_ref[0] = kr.astype(BF16)
    vg_ref[0, 0:KV_W, :] = v.T.astype(BF16)
    vg_ref[0, KV_W:KV_W + ONES_ROWS, :] = jnp.ones((ONES_ROWS, m), BF16)
    kir = rot(OFF_KI)[:, :IDX_DIM]
    kif_ref[0] = kir
    kib_ref[0] = kir.astype(BF16)
    wi_ref[0] = (h[:, OFF_WI:OFF_WI + LANES] * (IDX_HEADS ** -0.5)).T[0:IDX_HEADS, :]


def _proj_call(x, w, cos, sin, conva, lng, lnb, wm, bsr, p1=None, p2=None, *, tm, t_dec=0):
    bsz, tlen, _ = x.shape
    stateful = t_dec
    nt = tlen // tm
    grid = (bsz, nt)

    def row_blk(width):
        return pl.BlockSpec((1, tm, width), lambda b, j: (b, j, 0))

    def head_blk(nh):
        return pl.BlockSpec((1, nh, tm, HEAD_DIM), lambda b, j: (b, 0, j, 0))

    tab = pl.BlockSpec((tm, LANES), lambda b, j: (j, 0))
    in_specs = [row_blk(D_MODEL), _vmem_full(), tab, tab] + [_vmem_full()] * 5
    args = [x, w, cos, sin, conva, lng, lnb, wm, bsr]
    if stateful:
        in_specs += [_vmem_full(), _vmem_full()]
        args += [p1, p2]
    out_shape = [
        jax.ShapeDtypeStruct((bsz, tlen, A_W), BF16),
        jax.ShapeDtypeStruct((bsz, tlen, C_W), BF16),
        jax.ShapeDtypeStruct((bsz, B_HEADS, tlen, HEAD_DIM), BF16),
        jax.ShapeDtypeStruct((bsz, IDX_HEADS, tlen, IDX_DIM), BF16),
        jax.ShapeDtypeStruct((bsz, IDX_HEADS, tlen), F32),
        jax.ShapeDtypeStruct((bsz, tlen, KV_W), F32),
        jax.ShapeDtypeStruct((bsz, tlen, KV_W), F32),
        jax.ShapeDtypeStruct((bsz, tlen, IDX_DIM), F32),
        jax.ShapeDtypeStruct((bsz, tlen, KV_W), BF16),
        jax.ShapeDtypeStruct((bsz, KV_W + ONES_ROWS, tlen), BF16),
        jax.ShapeDtypeStruct((bsz, tlen, IDX_DIM), BF16),
    ]
    out_specs = [row_blk(A_W), row_blk(C_W), head_blk(B_HEADS), head_blk(IDX_HEADS),
                 pl.BlockSpec((1, IDX_HEADS, tm), lambda b, j: (b, 0, j)),
                 row_blk(KV_W), row_blk(KV_W), row_blk(IDX_DIM), row_blk(KV_W),
                 pl.BlockSpec((1, KV_W + ONES_ROWS, tm), lambda b, j: (b, 0, j)), row_blk(IDX_DIM)]
    if stateful:
        out_shape += [jax.ShapeDtypeStruct((bsz, tlen, A_W), F32), jax.ShapeDtypeStruct((bsz, tlen, C_W), F32)]
        out_specs += [row_blk(A_W), row_blk(C_W)]
    else:
        out_shape += [jax.ShapeDtypeStruct((bsz, CONV_W - 1, A_W), F32)]
        out_specs += [pl.BlockSpec((1, CONV_W - 1, A_W), lambda b, j: (b, 0, 0))]
    return pl.pallas_call(
        functools.partial(_proj_body, m=tm, stateful=stateful),
        grid=grid,
        in_specs=in_specs,
        out_specs=out_specs,
        out_shape=out_shape,
        scratch_shapes=[pltpu.VMEM((SUBLANES + tm, A_W), F32)],
        compiler_params=pltpu.CompilerParams(
            dimension_semantics=("arbitrary", "arbitrary"), vmem_limit_bytes=VMEM_LIMIT),
        name="proj_dec" if stateful else "proj",
    )(*args)


def _kth_largest_key(count_ge, ksel, shape):
    thr = jnp.where(count_ge(jnp.zeros(shape, I32)) >= ksel, 0, INT_MIN).astype(I32)

    def bit_step(it, thr):
        cand = thr + lax.shift_left(jnp.int32(1), 30 - it)
        return jnp.where(count_ge(cand) >= ksel, cand, thr)

    return lax.fori_loop(0, 31, bit_step, thr)


def _tie_cutoff(count_tied_below, need, nbits, shape):
    def bit_step(it, cut):
        cand = cut + lax.shift_left(jnp.int32(1), nbits - 1 - it)
        return jnp.where(count_tied_below(cand) <= need - 1, cand, cut)

    return lax.fori_loop(0, nbits, bit_step, jnp.zeros(shape, I32))


def _dsa_prompt_body(qi_ref, w_ref, ki_ref, q_ref, k_ref, v_ref, o_ref, sk_ref, hk_ref, qbd_ref, m_ref, l_ref, acc_ref,
                     *, qb, ck, ca, ksel, idx_bits):
    i = pl.program_id(1)
    nck = (i * qb) // ck + 1
    cols1 = (1, qb)

    qi_all = qi_ref[0].reshape(IDX_HEADS * qb, IDX_DIM)
    w = w_ref[0]
    diff = lax.broadcasted_iota(I32, (ck, qb), 0) - lax.broadcasted_iota(I32, (ck, qb), 1)

    def score_chunk(c, carry):
        off = pl.multiple_of(c * ck, ck)
        d = lax.dot_general(ki_ref[0, pl.ds(off, ck), :], qi_all, _NT, preferred_element_type=F32)
        sc = jnp.zeros((ck, qb), F32)
        for hh in range(IDX_HEADS):
            sc = sc + jnp.maximum(d[:, hh * qb:(hh + 1) * qb], 0.0) * w[hh:hh + 1, :]
        sc = jnp.where(diff <= i * qb - off, sc, -jnp.inf)
        sk_ref[pl.ds(off, ck), :] = _sortable_key(sc)
        return carry

    lax.fori_loop(0, nck, score_chunk, 0)

    def count_keys(pred):
        def body(c, acc):
            off = pl.multiple_of(c * ck, ck)
            hit = jnp.where(pred(sk_ref[pl.ds(off, ck), :], off), 1, 0)
            return acc + jnp.sum(hit.reshape(ck // SUBLANES, SUBLANES, qb), axis=0)

        acc = lax.fori_loop(0, nck, body, jnp.zeros((SUBLANES, qb), I32))
        return jnp.sum(acc, axis=0, keepdims=True)

    def count_ge(cand):
        return count_keys(lambda blk, off: blk >= cand)

    pack = 2 * SUBLANES

    def count_half(cand):
        nacc = 4

        def body(c, accs):
            off = pl.multiple_of(c * ck, ck)
            hit = jnp.where(hk_ref[pl.ds(off, ck), :] >= cand, jnp.int16(1), jnp.int16(0))
            accs = list(accs)
            for j in range(ck // pack):
                accs[j % nacc] = accs[j % nacc] + hit[j * pack:(j + 1) * pack]
            return tuple(accs)

        accs = lax.fori_loop(0, nck, body, tuple(jnp.zeros((pack, qb), I16) for _ in range(nacc)))
        total = accs[0].astype(I32)
        for a in accs[1:]:
            total = total + a.astype(I32)
        return jnp.sum(total, axis=0, keepdims=True)

    def kth_half():
        thr16 = jnp.where(count_half(jnp.zeros(cols1, I16)) >= ksel, 0, -32768).astype(I16)

        def bit_step(it, t):
            cand = t + lax.shift_left(jnp.int32(1), 14 - it).astype(I16)
            return jnp.where(count_half(cand) >= ksel, cand, t)

        return lax.fori_loop(0, 15, bit_step, thr16)

    def fill_hi(c, carry):
        off = pl.multiple_of(c * ck, ck)
        hk_ref[pl.ds(off, ck), :] = lax.shift_right_arithmetic(sk_ref[pl.ds(off, ck), :], 16).astype(I16)
        return carry

    lax.fori_loop(0, nck, fill_hi, 0)
    thr_hi = kth_half().astype(I32)

    def fill_lo(c, carry):
        off = pl.multiple_of(c * ck, ck)
        key = sk_ref[pl.ds(off, ck), :]
        hi = lax.shift_right_arithmetic(key, 16)
        lo = (key & 0xFFFF) - 32768
        hk_ref[pl.ds(off, ck), :] = jnp.where(hi == thr_hi, lo, jnp.where(hi > thr_hi, 32767, -32768)).astype(I16)
        return carry

    lax.fori_loop(0, nck, fill_lo, 0)
    thr = lax.shift_left(thr_hi, 16) + (kth_half().astype(I32) + 32768)

    n_ge = count_ge(thr)
    tie_cols = (n_ge > ksel) & (thr > NEG_INF_KEY)

    @pl.when(jnp.max(tie_cols.astype(I32)) > 0)
    def _():
        need = ksel - count_ge(thr + 1)
        kpos = lax.broadcasted_iota(I32, (ck, qb), 0)

        def count_tied_below(cand):
            return count_keys(lambda blk, off: (blk == thr) & (kpos + off < cand))

        cut = _tie_cutoff(count_tied_below, need, idx_bits, cols1)

        def demote(c, carry):
            off = pl.multiple_of(c * ck, ck)
            blk = sk_ref[pl.ds(off, ck), :]
            sk_ref[pl.ds(off, ck), :] = jnp.where((blk == thr) & (kpos + off > cut), blk - 1, blk)
            return carry

        lax.fori_loop(0, nck, demote, 0)

    thr_sel = jnp.maximum(thr, NEG_INF_KEY + 1)

    m_ref[...] = jnp.full(m_ref.shape, NEG_BIG, F32)
    l_ref[...] = jnp.zeros(l_ref.shape, F32)
    acc_ref[...] = jnp.zeros(acc_ref.shape, F32)

    zero_half = jnp.zeros((qb, HEAD_DIM), BF16)
    for hh in range(B_HEADS):
        halves = [zero_half] * N_KV
        halves[hh // GROUP] = q_ref[0, hh]
        qbd_ref[hh * qb:(hh + 1) * qb, :] = jnp.concatenate(halves, axis=1)
    nq = B_HEADS * qb
    half = GROUP * qb

    def attend_chunk(c, carry):
        off = pl.multiple_of(c * ca, ca)
        bias = jnp.where(sk_ref[pl.ds(off, ca), :] >= thr_sel, 0.0, NEG_BIG)
        bias = jnp.concatenate([bias] * B_HEADS, axis=1)
        s = lax.dot_general(k_ref[0, pl.ds(off, ca), :], qbd_ref[...], _NT, preferred_element_type=F32) + bias
        m_old = m_ref[...]
        m_new = jnp.maximum(m_old, jnp.max(s, axis=0, keepdims=True))
        alpha = jnp.exp(m_old - m_new)
        p = jnp.exp((s - m_new).astype(BF16))
        pv = jnp.dot(v_ref[0, :, pl.ds(off, ca)], p, preferred_element_type=F32)
        l_ref[...] = alpha * l_ref[...] + pv[KV_W:KV_W + 1, :]
        pv = jnp.concatenate([pv[g * HEAD_DIM:(g + 1) * HEAD_DIM, g * half:(g + 1) * half] for g in range(N_KV)], axis=1)
        acc_ref[...] = alpha * acc_ref[...] + pv
        m_ref[...] = m_new
        return carry

    lax.fori_loop(0, (i * qb) // ca + 1, attend_chunk, 0)

    o = acc_ref[...] / l_ref[...]
    for hh in range(B_HEADS):
        o_ref[0, :, hh * HEAD_DIM:(hh + 1) * HEAD_DIM] = o[:, hh * qb:(hh + 1) * qb].T.astype(o_ref.dtype)


def _dsa_prompt_call(qih, wi, kib, qh, kg, vg):
    bsz, _, slen, _ = qh.shape
    qb = 128
    ck = 512
    ksel = min(TOPK_MAX, slen // 4)
    assert slen % ck == 0 and ck >= ksel
    idx_bits = max(1, int(slen - 1).bit_length())
    ca = ck
    body = functools.partial(_dsa_prompt_body, qb=qb, ck=ck, ca=ca, ksel=ksel, idx_bits=idx_bits)
    return pl.pallas_call(
        body,
        grid=(bsz, slen // qb),
        in_specs=[
            pl.BlockSpec((1, IDX_HEADS, qb, IDX_DIM), lambda b, i: (b, 0, i, 0)),
            pl.BlockSpec((1, IDX_HEADS, qb), lambda b, i: (b, 0, i)),
            pl.BlockSpec((1, slen, IDX_DIM), lambda b, i: (b, 0, 0)),
            pl.BlockSpec((1, B_HEADS, qb, HEAD_DIM), lambda b, i: (b, 0, i, 0)),
            pl.BlockSpec((1, slen, KV_W), lambda b, i: (b, 0, 0)),
            pl.BlockSpec((1, KV_W + ONES_ROWS, slen), lambda b, i: (b, 0, 0)),
        ],
        out_specs=pl.BlockSpec((1, qb, B_W), lambda b, i: (b, i, 0)),
        out_shape=jax.ShapeDtypeStruct((bsz, slen, B_W), BF16),
        scratch_shapes=[
            pltpu.VMEM((slen, qb), I32),
            pltpu.VMEM((slen, qb), I16),
            pltpu.VMEM((B_HEADS * qb, KV_W), BF16),
            pltpu.VMEM((1, B_HEADS * qb), F32),
            pltpu.VMEM((1, B_HEADS * qb), F32),
            pltpu.VMEM((HEAD_DIM, B_HEADS * qb), F32),
        ],
        compiler_params=pltpu.CompilerParams(
            dimension_semantics=("arbitrary", "arbitrary"), vmem_limit_bytes=VMEM_LIMIT),
        name="dsa_prompt",
    )(qih, wi, kib, qh, kg, vg)


def _dsa_dec_body(pt_ref, qi_ref, w_ref, q_ref, kin_ref, kn_ref, vn_ref, cki_hbm, ck_hbm, cv_hbm,
                  o_ref, kibuf, kbuf, vbuf, sems, sk_ref, *, layer, n_pages, t_dec, ksel, idx_bits):
    b = pl.program_id(0)
    past = n_pages * PAGE_SIZE
    tpad = SUBLANES
    total = past + LANES

    def page_copies(p):
        phys = pt_ref[b, p]
        dst = pl.ds(pl.multiple_of(p * PAGE_SIZE, PAGE_SIZE), PAGE_SIZE)
        return (pltpu.make_async_copy(cki_hbm.at[layer, phys], kibuf.at[:, dst], sems.at[0, p]),
                pltpu.make_async_copy(ck_hbm.at[layer, phys], kbuf.at[:, dst], sems.at[1, p]),
                pltpu.make_async_copy(cv_hbm.at[layer, phys], vbuf.at[:, dst], sems.at[2, p]))

    def start_page(p, carry):
        for cp in page_copies(p):
            cp.start()
        return carry

    def wait_page(p, carry):
        for cp in page_copies(p):
            cp.wait()
        return carry

    lax.fori_loop(0, n_pages, start_page, 0)
    lax.fori_loop(0, n_pages, wait_page, 0)

    rows1 = (tpad, 1)
    qi = qi_ref[0]
    w = w_ref[0]

    def scores(d):
        r = jnp.maximum(d, 0.0) * w
        n = r.shape[-1]
        return jnp.sum(r.reshape(IDX_HEADS, tpad, n), axis=0)

    d_past = jnp.dot(qi, kibuf[...].astype(BF16), preferred_element_type=F32)
    sk_ref[:, 0:past] = _sortable_key(scores(d_past))
    trow = lax.broadcasted_iota(I32, (tpad, LANES), 0)
    tcol = lax.broadcasted_iota(I32, (tpad, LANES), 1)
    d_new = lax.dot_general(qi, kin_ref[0], _NT, preferred_element_type=F32)
    sc_new = jnp.where((tcol <= trow) & (tcol < t_dec), scores(d_new), -jnp.inf)
    sk_ref[:, past:total] = _sortable_key(sc_new)

    def count_ge(cand):
        return jnp.sum(jnp.where(sk_ref[...] >= cand, 1, 0), axis=1, keepdims=True)

    thr = _kth_largest_key(count_ge, ksel, rows1)
    n_ge = count_ge(thr)
    real_row = lax.broadcasted_iota(I32, rows1, 0) < t_dec
    tie_rows = (n_ge > ksel) & (thr > NEG_INF_KEY) & real_row

    @pl.when(jnp.max(tie_rows.astype(I32)) > 0)
    def _():
        need = ksel - count_ge(thr + 1)
        lane = lax.broadcasted_iota(I32, (tpad, total), 1)

        def count_tied_below(cand):
            return jnp.sum(jnp.where((sk_ref[...] == thr) & (lane < cand), 1, 0), axis=1, keepdims=True)

        cut = _tie_cutoff(count_tied_below, need, idx_bits, rows1)
        blk = sk_ref[...]
        sk_ref[...] = jnp.where((blk == thr) & (lane > cut), blk - 1, blk)

    thr_sel = jnp.maximum(thr, NEG_INF_KEY + 1)
    bias = jnp.where(sk_ref[...] >= thr_sel, 0.0, NEG_BIG)

    kt_past = kbuf[...].astype(BF16)
    vt_past = vbuf[...].astype(BF16)
    for g in range(N_KV):
        qg = q_ref[0, g]
        s = jnp.concatenate([
            jnp.dot(qg, kt_past, preferred_element_type=F32),
            lax.dot_general(qg, kn_ref[0], _NT, preferred_element_type=F32)], axis=1)
        s = (s.reshape(GROUP, tpad, total) + bias[None]).reshape(GROUP * tpad, total)
        p = jnp.exp(s - jnp.max(s, axis=1, keepdims=True))
        l = jnp.sum(p, axis=1, keepdims=True)
        pb = p.astype(BF16)
        o = (lax.dot_general(pb[:, 0:past], vt_past, _NT, preferred_element_type=F32)
             + jnp.dot(pb[:, past:total], vn_ref[0], preferred_element_type=F32))
        o_ref[0, g] = (o / l)[:, g * HEAD_DIM:(g + 1) * HEAD_DIM]


def _dsa_dec_call(page_table, qi_l, w_l, q_l, ki_new, k_new, v_new, cache_kidx, cache_k, cache_v, *, layer, t_dec):
    bd, n_pages = page_table.shape
    past = n_pages * PAGE_SIZE
    total = past + LANES
    ksel = min(TOPK_MAX, (past + t_dec) // 4)
    idx_bits = int(total - 1).bit_length()
    tpad = SUBLANES
    body = functools.partial(_dsa_dec_body, layer=layer, n_pages=n_pages, t_dec=t_dec, ksel=ksel, idx_bits=idx_bits)
    any_spec = pl.BlockSpec(memory_space=pl.ANY)
    grid_spec = pltpu.PrefetchScalarGridSpec(
        num_scalar_prefetch=1,
        grid=(bd,),
        in_specs=[
            pl.BlockSpec((1, IDX_HEADS * tpad, IDX_DIM), lambda b, pt: (b, 0, 0)),
            pl.BlockSpec((1, IDX_HEADS * tpad, 1), lambda b, pt: (b, 0, 0)),
            pl.BlockSpec((1, N_KV, GROUP * tpad, KV_W), lambda b, pt: (b, 0, 0, 0)),
            pl.BlockSpec((1, LANES, IDX_DIM), lambda b, pt: (b, 0, 0)),
            pl.BlockSpec((1, LANES, KV_W), lambda b, pt: (b, 0, 0)),
            pl.BlockSpec((1, LANES, KV_W), lambda b, pt: (b, 0, 0)),
            any_spec, any_spec, any_spec,
        ],
        out_specs=pl.BlockSpec((1, N_KV, GROUP * tpad, HEAD_DIM), lambda b, pt: (b, 0, 0, 0)),
        scratch_shapes=[
            pltpu.VMEM((IDX_DIM, past), F32),
            pltpu.VMEM((KV_W, past), F32),
            pltpu.VMEM((KV_W, past), F32),
            pltpu.SemaphoreType.DMA((3, n_pages)),
            pltpu.VMEM((tpad, total), I32),
        ],
    )
    return pl.pallas_call(
        body,
        grid_spec=grid_spec,
        out_shape=jax.ShapeDtypeStruct((bd, N_KV, GROUP * tpad, HEAD_DIM), F32),
        compiler_params=pltpu.CompilerParams(dimension_semantics=("arbitrary",), vmem_limit_bytes=VMEM_LIMIT),
        name="dsa_dec",
    )(page_table, qi_l, w_l, q_l, ki_new, k_new, v_new, cache_kidx, cache_k, cache_v)


def _dsa_dec(qih, wi, qh, kf, vf, kib, page_table, cache_kidx, cache_k2, cache_v2, *, layer, bd, td):
    tpad = SUBLANES
    md = bd * td

    def head_rows(xh, nh):
        xh = xh[0].reshape(nh, bd, td, HEAD_DIM).transpose(1, 0, 2, 3)
        return jnp.pad(xh, ((0, 0), (0, 0), (0, tpad - td), (0, 0)))

    qi_l = head_rows(qih, IDX_HEADS).reshape(bd, IDX_HEADS * tpad, IDX_DIM)
    w_l = jnp.pad(wi[0].reshape(IDX_HEADS, bd, td).transpose(1, 0, 2), ((0, 0), (0, 0), (0, tpad - td)))
    w_l = w_l.reshape(bd, IDX_HEADS * tpad, 1)
    q_g = head_rows(qh, B_HEADS).reshape(bd, N_KV, GROUP * tpad, HEAD_DIM)
    zeros_q = jnp.zeros_like(q_g)
    q_l = jnp.stack([jnp.concatenate([q_g[:, 0], zeros_q[:, 0]], axis=-1),
                     jnp.concatenate([zeros_q[:, 1], q_g[:, 1]], axis=-1)], axis=1)
    pad_rows = ((0, 0), (0, LANES - td), (0, 0))
    ki_new = jnp.pad(kib[0].reshape(bd, td, IDX_DIM), pad_rows)
    k_new = jnp.pad(kf[0].astype(BF16).reshape(bd, td, KV_W), pad_rows)
    v_new = jnp.pad(vf[0].astype(BF16).reshape(bd, td, KV_W), pad_rows)
    o_l = _dsa_dec_call(page_table, qi_l, w_l, q_l, ki_new, k_new, v_new, cache_kidx, cache_k2, cache_v2,
                        layer=layer, t_dec=td)
    o_g = o_l.reshape(bd, N_KV, GROUP, tpad, HEAD_DIM)[:, :, :, :td]
    return o_g.transpose(0, 3, 1, 2, 4).reshape(1, md, B_W).astype(BF16)


FF_TILE = 256


def _post_body(*refs, m, alpha, t_dec):
    if t_dec:
        (x_ref, a_ref, b_ref, c_ref, wo_ref, g1_ref, b1_ref, wup_ref, cf_ref, wdn_ref, g2_ref, b2_ref,
         p1_ref, p2_ref, y_ref, st_ref, hist_ref) = refs
    else:
        (x_ref, a_ref, b_ref, c_ref, wo_ref, g1_ref, b1_ref, wup_ref, cf_ref, wdn_ref, g2_ref, b2_ref,
         y_ref, st_ref, hist_ref) = refs

    @pl.when(pl.program_id(1) == 0)
    def _():
        hist_ref[0:SUBLANES, :] = jnp.zeros((SUBLANES, 2 * D_FF), F32)

    mix = (jnp.dot(a_ref[0], wo_ref[0:A_W, :], preferred_element_type=F32)
           + jnp.dot(b_ref[0], wo_ref[A_W:A_W + B_W, :], preferred_element_type=F32)
           + jnp.dot(c_ref[0], wo_ref[A_W + B_W:, :], preferred_element_type=F32))
    x1 = _layer_norm(alpha * x_ref[0] + mix, g1_ref[...], b1_ref[...])
    x1b = x1.astype(BF16)

    t = lax.broadcasted_iota(I32, (m, 1), 0) % t_dec if t_dec else None
    f = jnp.zeros((m, D_MODEL), F32)
    for j in range(D_FF // FF_TILE):
        gcols = slice(j * FF_TILE, (j + 1) * FF_TILE)
        ucols = slice(D_FF + j * FF_TILE, D_FF + (j + 1) * FF_TILE)
        halves = []
        for cols in (gcols, ucols):
            hu = jnp.dot(x1b, wup_ref[:, cols], preferred_element_type=F32)
            hist_ref[pl.ds(SUBLANES, m), cols] = hu
            state = (t, p1_ref[:, cols], p2_ref[:, cols]) if t_dec else None
            halves.append(_causal_conv3(hu, hist_ref, cols, cf_ref, m, state))
        gate, up = halves
        act = gate * (1.0 / (1.0 + jnp.exp(-gate))) * up
        f = f + jnp.dot(act.astype(BF16), wdn_ref[gcols, :], preferred_element_type=F32)
    y_ref[0] = _layer_norm(alpha * x1 + f, g2_ref[...], b2_ref[...])

    if t_dec:
        st_ref[0] = hist_ref[pl.ds(SUBLANES, m), :]
    else:
        last = hist_ref[pl.ds(SUBLANES + m - 2, 2), :]
        st_ref[0] = last
        hist_ref[SUBLANES - 2:SUBLANES, :] = last


def _post_call(x, a, bo, c, wo, g1, b1, wup, cf, wdn, g2, b2, p1=None, p2=None, *, tm, alpha, t_dec=0):
    bsz, tlen, _ = x.shape
    nt = tlen // tm

    def row_blk(width):
        return pl.BlockSpec((1, tm, width), lambda b, j: (b, j, 0))

    in_specs = [row_blk(D_MODEL), row_blk(A_W), row_blk(B_W), row_blk(C_W)] + [_vmem_full()] * 8
    args = [x, a, bo, c, wo, g1, b1, wup, cf, wdn, g2, b2]
    if t_dec:
        in_specs += [_vmem_full(), _vmem_full()]
        args += [p1, p2]
        st_shape = jax.ShapeDtypeStruct((bsz, tlen, 2 * D_FF), F32)
        st_spec = row_blk(2 * D_FF)
    else:
        st_shape = jax.ShapeDtypeStruct((bsz, CONV_W - 1, 2 * D_FF), F32)
        st_spec = pl.BlockSpec((1, CONV_W - 1, 2 * D_FF), lambda b, j: (b, 0, 0))
    return pl.pallas_call(
        functools.partial(_post_body, m=tm, alpha=alpha, t_dec=t_dec),
        grid=(bsz, nt),
        in_specs=in_specs,
        out_specs=[row_blk(D_MODEL), st_spec],
        out_shape=[jax.ShapeDtypeStruct((bsz, tlen, D_MODEL), F32), st_shape],
        scratch_shapes=[pltpu.VMEM((SUBLANES + tm, 2 * D_FF), F32)],
        compiler_params=pltpu.CompilerParams(
            dimension_semantics=("arbitrary", "arbitrary"), vmem_limit_bytes=VMEM_LIMIT),
        name="post_dec" if t_dec else "post",
    )(*args)


def _pad_cols(w, width):
    return jnp.pad(w, ((0, 0), (0, width - w.shape[1])))


def _rope_tables(pos):
    inv = ROPE_THETA ** (-jnp.arange(ROT_HALF, dtype=F32) / ROT_HALF)
    ang = pos.astype(F32)[:, None] * inv[None, :]
    cos, sin = jnp.cos(ang), jnp.sin(ang)
    n = pos.shape[0]
    rest = HEAD_DIM - ROT_DIM
    cos_h = jnp.concatenate([cos, cos, jnp.ones((n, rest), F32)], axis=1)
    sin_h = jnp.concatenate([-sin, sin, jnp.zeros((n, rest), F32)], axis=1)
    reps = LANES // HEAD_DIM
    return jnp.tile(cos_h, (1, reps)), jnp.tile(sin_h, (1, reps))


def kernel(x_prompt, x_sample, cache_k, cache_v, cache_kidx, page_table, state_conv, state_ffn_conv, w_in, conv_a,
           ln_cv_g, ln_cv_b, w_sp, b_sp, w_o, ln1_g, ln1_b, w_up, conv_f, w_down, ln2_g, ln2_b):
    depth = w_in.shape[0]
    alpha = (2 * depth) ** 0.25
    bp, sp, _ = x_prompt.shape
    bd, td, _ = x_sample.shape
    n_pages = page_table.shape[1]
    past = n_pages * PAGE_SIZE
    md = bd * td
    tpad = SUBLANES
    assert sp % 512 == 0 and md % CHUNK == 0 and CHUNK % td == 0 and td <= tpad and td <= CHUNK
    tm_proj = 512
    tm_post = 256

    cos_p, sin_p = _rope_tables(jnp.arange(sp, dtype=I32))
    cos_s, sin_s = _rope_tables(jnp.tile(past + jnp.arange(td, dtype=I32), bd))

    n_pool = cache_k.shape[1]
    cache_k2 = cache_k.transpose(0, 1, 3, 4, 2).reshape(depth, n_pool, KV_W, PAGE_SIZE)
    cache_v2 = cache_v.transpose(0, 1, 3, 4, 2).reshape(depth, n_pool, KV_W, PAGE_SIZE)
    cache_kidx = cache_kidx.transpose(0, 1, 3, 2)

    tril = jnp.tril(jnp.ones((CHUNK, CHUNK), bool))
    eye_b = jnp.eye(CHUNK // td, dtype=F32)
    t_of_row = jnp.arange(md) % td

    hp, hs = x_prompt, x_sample.reshape(1, md, D_MODEL)
    outs = [[] for _ in range(11)]
    for l in range(depth):
        w = w_in[l]
        w_pad = jnp.concatenate([
            w[:, :OFF_KI],
            _pad_cols(w[:, OFF_KI:OFF_KI + IDX_DIM], LANES),
            _pad_cols(w[:, OFF_KI + IDX_DIM:OFF_KI + IDX_DIM + IDX_HEADS], LANES),
            w[:, OFF_KI + IDX_DIM + IDX_HEADS:]], axis=1).astype(BF16)
        lng, lnb = ln_cv_g[l][None], ln_cv_b[l][None]
        wo_b, wup_b, wdn_b = w_o[l].astype(BF16), w_up[l].astype(BF16), w_down[l].astype(BF16)
        g1, b1, g2, b2 = ln1_g[l][None], ln1_b[l][None], ln2_g[l][None], ln2_b[l][None]

        wm_p = jnp.where(tril[None], w_sp[l], 0).astype(BF16)
        bsr_p = jnp.repeat(b_sp[l].T, HEAD_DIM, axis=1)
        (a_o, c_o, qh, qih, wi, kf, vf, kif, kg, vg, kib, conv_new) = _proj_call(
            hp, w_pad, cos_p, sin_p, conv_a[l], lng, lnb, wm_p, bsr_p, tm=tm_proj)
        b_o = _dsa_prompt_call(qih, wi, kib, qh, kg, vg)
        hp, ffn_new = _post_call(hp, a_o, b_o, c_o, wo_b, g1, b1, wup_b, conv_f[l], wdn_b, g2, b2,
                                 tm=tm_post, alpha=alpha)
        outs[0].append(kf.reshape(bp, sp, N_KV, HEAD_DIM))
        outs[2].append(vf.reshape(bp, sp, N_KV, HEAD_DIM))
        outs[4].append(kif)
        outs[6].append(conv_new)
        outs[8].append(ffn_new)

        ws_small = jnp.where(tril[None, :td, :td], w_sp[l][:, :td, :td], 0)
        wm_s = jnp.einsum("ab,hts->hatbs", eye_b, ws_small).reshape(C_HEADS, CHUNK, CHUNK).astype(BF16)
        bsr_s = jnp.repeat(b_sp[l][:, :td].T, HEAD_DIM, axis=1)[jnp.arange(CHUNK) % td]
        st = state_conv[l]
        p1 = jnp.repeat(st[:, 1], td, axis=0)
        p2 = jnp.where((t_of_row == 0)[:, None], jnp.repeat(st[:, 0], td, axis=0), jnp.repeat(st[:, 1], td, axis=0))
        (a_o, c_o, qh, qih, wi, kf, vf, kif, kg, vg, kib, z_all, vn_all) = _proj_call(
            hs, w_pad, cos_s, sin_s, conv_a[l], lng, lnb, wm_s, bsr_s, p1, p2, tm=md, t_dec=td)

        b_o = _dsa_dec(qih, wi, qh, kf, vf, kib, page_table, cache_kidx, cache_k2, cache_v2, layer=l, bd=bd, td=td)

        fs = state_ffn_conv[l]
        f1 = jnp.repeat(fs[:, 1], td, axis=0)
        f2 = jnp.where((t_of_row == 0)[:, None], jnp.repeat(fs[:, 0], td, axis=0), jnp.repeat(fs[:, 1], td, axis=0))
        hs, ffn_all = _post_call(hs, a_o, b_o, c_o, wo_b, g1, b1, wup_b, conv_f[l], wdn_b, g2, b2, f1, f2,
                                 tm=md, alpha=alpha, t_dec=td)
        outs[1].append(kf.reshape(bd, td, N_KV, HEAD_DIM))
        outs[3].append(vf.reshape(bd, td, N_KV, HEAD_DIM))
        outs[5].append(kif.reshape(bd, td, IDX_DIM))
        outs[7].append(z_all.reshape(bd, td, A_W)[:, td - (CONV_W - 1):])
        outs[9].append(ffn_all.reshape(bd, td, 2 * D_FF)[:, td - (CONV_W - 1):])
        outs[10].append(vn_all.reshape(bd, td, C_W))

    return (hp, hs.reshape(bd, td, D_MODEL)) + tuple(jnp.stack(o) for o in outs)
```

```python
import functools

import jax
import jax.numpy as jnp
import numpy as np
from jax import lax
from jax.experimental import pallas as pl
from jax.experimental.pallas import tpu as pltpu

F32 = jnp.float32
BF16 = jnp.bfloat16
I32 = jnp.int32

D_MODEL = 1024
HEAD_DIM = 64
A_W = D_MODEL // 4
B_W = D_MODEL // 2
C_W = D_MODEL // 4
B_HEADS = B_W // HEAD_DIM
C_HEADS = C_W // HEAD_DIM
N_KV = 2
KV_W = N_KV * HEAD_DIM
GROUP = B_HEADS // N_KV
IDX_HEADS = 8
IDX_DIM = 64
TOPK_MAX = 256
ROT_DIM = HEAD_DIM // 4
ROT_HALF = ROT_DIM // 2
ROPE_THETA = 500000.0
CONV_W = 3
CHUNK = 128
D_FF = 2816
PAGE_SIZE = 128
LN_EPS = 1e-5

LANES = 128
ONES_ROWS = 16
SUBLANES = 8
VMEM_LIMIT = 56 * 1024 * 1024

OFF_GB = 0
OFF_GC = OFF_GB + A_W
OFF_HA = OFF_GC + A_W
OFF_Q = OFF_HA + A_W
OFF_K = OFF_Q + B_W
OFF_V = OFF_K + KV_W
OFF_QI = OFF_V + KV_W
OFF_KI = OFF_QI + IDX_HEADS * IDX_DIM
OFF_WI = OFF_KI + LANES
OFF_U = OFF_WI + LANES
OFF_VC = OFF_U + C_W
N_IN_PAD = OFF_VC + C_W

NEG_BIG = -0.7 * float(np.finfo(np.float32).max)
INT_MIN = -(2 ** 31)
NEG_INF_KEY = int(np.int32(np.uint32(0xFF800000) ^ np.uint32(0x7FFFFFFF)))

_NT = (((1,), (1,)), ((), ()))


def _vmem_full():
    return pl.BlockSpec(memory_space=pltpu.VMEM)


def _layer_norm(x, g, b):
    mu = jnp.mean(x, axis=-1, keepdims=True)
    xc = x - mu
    var = jnp.mean(xc * xc, axis=-1, keepdims=True)
    return xc * lax.rsqrt(var + LN_EPS) * g + b


def _sortable_key(x):
    k = lax.bitcast_convert_type(x, I32)
    return k ^ (lax.shift_right_arithmetic(k, 31) & 0x7FFFFFFF)


def _rotary(seg, cos, sin, first_half):
    n = seg.shape[-1]
    fwd = pltpu.roll(seg, n - ROT_HALF, 1)
    bwd = pltpu.roll(seg, ROT_HALF, 1)
    return seg * cos + jnp.where(first_half, fwd, bwd) * sin


def _causal_conv3(cur, hist_ref, cols, w_ref, m, state=None):
    z1 = hist_ref[pl.ds(SUBLANES - 1, m), cols]
    z2 = hist_ref[pl.ds(SUBLANES - 2, m), cols]
    if state is not None:
        t, p1, p2 = state
        z1 = jnp.where(t >= 1, z1, p1)
        z2 = jnp.where(t >= 2, z2, p2)
    return z2 * w_ref[0:1, cols] + z1 * w_ref[1:2, cols] + cur * w_ref[2:3, cols]


def _proj_body(*refs, m, stateful):
    if stateful:
        (x_ref, w_ref, cos_ref, sin_ref, conva_ref, lng_ref, lnb_ref, wm_ref, bsr_ref, p1_ref, p2_ref,
         a_ref, c_ref, qh_ref, qih_ref, wi_ref, kf_ref, vf_ref, kif_ref, kg_ref, vg_ref, kib_ref, z_ref, vn_ref,
         hist_ref) = refs
    else:
        (x_ref, w_ref, cos_ref, sin_ref, conva_ref, lng_ref, lnb_ref, wm_ref, bsr_ref,
         a_ref, c_ref, qh_ref, qih_ref, wi_ref, kf_ref, vf_ref, kif_ref, kg_ref, vg_ref, kib_ref, z_ref,
         hist_ref) = refs

    first_tile = pl.program_id(1) == 0

    @pl.when(first_tile)
    def _():
        hist_ref[0:SUBLANES, :] = jnp.zeros((SUBLANES, A_W), F32)

    x = x_ref[0].astype(BF16)
    h = jnp.dot(x, w_ref[...], preferred_element_type=F32)

    z = h[:, OFF_GC:OFF_GC + A_W] * h[:, OFF_HA:OFF_HA + A_W]
    hist_ref[pl.ds(SUBLANES, m), :] = z
    state = None
    if stateful:
        t = lax.broadcasted_iota(I32, (m, 1), 0) % stateful
        state = (t, p1_ref[...], p2_ref[...])
    y = _causal_conv3(z, hist_ref, slice(None), conva_ref, m, state)
    a_ref[0] = (h[:, OFF_GB:OFF_GB + A_W] * y).astype(BF16)
    if stateful:
        z_ref[0] = z
    else:
        z_ref[0] = z[m - 2:m, :]
        hist_ref[SUBLANES - 2:SUBLANES, :] = z[m - 2:m, :]

    vn = _layer_norm(h[:, OFF_VC:OFF_VC + C_W], lng_ref[...], lnb_ref[...])
    if stateful:
        vn_ref[0] = vn
    vnb = vn.astype(BF16)
    lane_head = lax.broadcasted_iota(I32, (CHUNK, C_W), 1) // HEAD_DIM
    for r in range(m // CHUNK):
        rows = slice(r * CHUNK, (r + 1) * CHUNK)
        mixed = jnp.zeros((CHUNK, C_W), F32)
        for hh in range(C_HEADS):
            part = jnp.dot(wm_ref[hh], vnb[rows], preferred_element_type=F32)
            mixed = jnp.where(lane_head == hh, part, mixed)
        c_ref[0, rows, :] = (h[rows, OFF_U:OFF_U + C_W] * (mixed + bsr_ref[...])).astype(BF16)

    cos = cos_ref[...]
    sin = sin_ref[...]
    first_half = (lax.broadcasted_iota(I32, (m, LANES), 1) % HEAD_DIM) < ROT_HALF

    def rot(off):
        return _rotary(h[:, off:off + LANES], cos, sin, first_half)

    for j in range(B_W // LANES):
        qr = (rot(OFF_Q + j * LANES) * (HEAD_DIM ** -0.5)).astype(BF16)
        qir = (rot(OFF_QI + j * LANES) * (IDX_DIM ** -0.5)).astype(BF16)
        for e in range(LANES // HEAD_DIM):
            hh = j * (LANES // HEAD_DIM) + e
            qh_ref[0, hh] = qr[:, e * HEAD_DIM:(e + 1) * HEAD_DIM]
            qih_ref[0, hh] = qir[:, e * HEAD_DIM:(e + 1) * HEAD_DIM]
    kr = rot(OFF_K)
    v = h[:, OFF_V:OFF_V + KV_W]
    kf_ref[0] = kr
    vf_ref[0] = v
    kg_ref[0] = kr.astype(BF16)
    vg_ref[0, 0:KV_W, :] = v.T.astype(BF16)
    vg_ref[0, KV_W:KV_W + ONES_ROWS, :] = jnp.ones((ONES_ROWS, m), BF16)
    kir = rot(OFF_KI)[:, :IDX_DIM]
    kif_ref[0] = kir
    kib_ref[0] = kir.astype(BF16)
    wi_ref[0] = (h[:, OFF_WI:OFF_WI + LANES] * (IDX_HEADS ** -0.5)).T[0:IDX_HEADS, :]


def _proj_call(x, w, cos, sin, conva, lng, lnb, wm, bsr, p1=None, p2=None, *, tm, t_dec=0):
    bsz, tlen, _ = x.shape
    stateful = t_dec
    nt = tlen // tm
    grid = (bsz, nt)

    def row_blk(width):
        return pl.BlockSpec((1, tm, width), lambda b, j: (b, j, 0))

    def head_blk(nh):
        return pl.BlockSpec((1, nh, tm, HEAD_DIM), lambda b, j: (b, 0, j, 0))

    tab = pl.BlockSpec((tm, LANES), lambda b, j: (j, 0))
    in_specs = [row_blk(D_MODEL), _vmem_full(), tab, tab] + [_vmem_full()] * 5
    args = [x, w, cos, sin, conva, lng, lnb, wm, bsr]
    if stateful:
        in_specs += [_vmem_full(), _vmem_full()]
        args += [p1, p2]
    out_shape = [
        jax.ShapeDtypeStruct((bsz, tlen, A_W), BF16),
        jax.ShapeDtypeStruct((bsz, tlen, C_W), BF16),
        jax.ShapeDtypeStruct((bsz, B_HEADS, tlen, HEAD_DIM), BF16),
        jax.ShapeDtypeStruct((bsz, IDX_HEADS, tlen, IDX_DIM), BF16),
        jax.ShapeDtypeStruct((bsz, IDX_HEADS, tlen), F32),
        jax.ShapeDtypeStruct((bsz, tlen, KV_W), F32),
        jax.ShapeDtypeStruct((bsz, tlen, KV_W), F32),
        jax.ShapeDtypeStruct((bsz, tlen, IDX_DIM), F32),
        jax.ShapeDtypeStruct((bsz, tlen, KV_W), BF16),
        jax.ShapeDtypeStruct((bsz, KV_W + ONES_ROWS, tlen), BF16),
        jax.ShapeDtypeStruct((bsz, tlen, IDX_DIM), BF16),
    ]
    out_specs = [row_blk(A_W), row_blk(C_W), head_blk(B_HEADS), head_blk(IDX_HEADS),
                 pl.BlockSpec((1, IDX_HEADS, tm), lambda b, j: (b, 0, j)),
                 row_blk(KV_W), row_blk(KV_W), row_blk(IDX_DIM), row_blk(KV_W),
                 pl.BlockSpec((1, KV_W + ONES_ROWS, tm), lambda b, j: (b, 0, j)), row_blk(IDX_DIM)]
    if stateful:
        out_shape += [jax.ShapeDtypeStruct((bsz, tlen, A_W), F32), jax.ShapeDtypeStruct((bsz, tlen, C_W), F32)]
        out_specs += [row_blk(A_W), row_blk(C_W)]
    else:
        out_shape += [jax.ShapeDtypeStruct((bsz, CONV_W - 1, A_W), F32)]
        out_specs += [pl.BlockSpec((1, CONV_W - 1, A_W), lambda b, j: (b, 0, 0))]
    return pl.pallas_call(
        functools.partial(_proj_body, m=tm, stateful=stateful),
        grid=grid,
        in_specs=in_specs,
        out_specs=out_specs,
        out_shape=out_shape,
        scratch_shapes=[pltpu.VMEM((SUBLANES + tm, A_W), F32)],
        compiler_params=pltpu.CompilerParams(
            dimension_semantics=("arbitrary", "arbitrary"), vmem_limit_bytes=VMEM_LIMIT),
        name="proj_dec" if stateful else "proj",
    )(*args)


def _kth_largest_key(count_ge, ksel, shape):
    thr = jnp.where(count_ge(jnp.zeros(shape, I32)) >= ksel, 0, INT_MIN).astype(I32)

    def bit_step(it, thr):
        cand = thr + lax.shift_left(jnp.int32(1), 30 - it)
        return jnp.where(count_ge(cand) >= ksel, cand, thr)

    return lax.fori_loop(0, 31, bit_step, thr)


def _tie_cutoff(count_tied_below, need, nbits, shape):
    def bit_step(it, cut):
        cand = cut + lax.shift_left(jnp.int32(1), nbits - 1 - it)
        return jnp.where(count_tied_below(cand) <= need - 1, cand, cut)

    return lax.fori_loop(0, nbits, bit_step, jnp.zeros(shape, I32))


def _dsa_prompt_body(qi_ref, w_ref, ki_ref, q_ref, k_ref, v_ref, o_ref, sk_ref, qbd_ref, m_ref, l_ref, acc_ref,
                     *, qb, ck, ca, ksel, idx_bits):
    i = pl.program_id(1)
    nck = (i * qb) // ck + 1
    cols1 = (1, qb)

    qi_all = qi_ref[0].reshape(IDX_HEADS * qb, IDX_DIM)
    w = w_ref[0]
    diff = lax.broadcasted_iota(I32, (ck, qb), 0) - lax.broadcasted_iota(I32, (ck, qb), 1)

    def score_chunk(c, carry):
        off = pl.multiple_of(c * ck, ck)
        d = lax.dot_general(ki_ref[0, pl.ds(off, ck), :], qi_all, _NT, preferred_element_type=F32)
        sc = jnp.zeros((ck, qb), F32)
        for hh in range(IDX_HEADS):
            sc = sc + jnp.maximum(d[:, hh * qb:(hh + 1) * qb], 0.0) * w[hh:hh + 1, :]
        sc = jnp.where(diff <= i * qb - off, sc, -jnp.inf)
        sk_ref[pl.ds(off, ck), :] = _sortable_key(sc)
        return carry

    lax.fori_loop(0, nck, score_chunk, 0)

    def count_keys(pred):
        def body(c, acc):
            off = pl.multiple_of(c * ck, ck)
            hit = jnp.where(pred(sk_ref[pl.ds(off, ck), :], off), 1, 0)
            return acc + jnp.sum(hit.reshape(ck // SUBLANES, SUBLANES, qb), axis=0)

        acc = lax.fori_loop(0, nck, body, jnp.zeros((SUBLANES, qb), I32))
        return jnp.sum(acc, axis=0, keepdims=True)

    def count_ge(cand):
        return count_keys(lambda blk, off: blk >= cand)

    thr = _kth_largest_key(count_ge, ksel, cols1)

    n_ge = count_ge(thr)
    tie_cols = (n_ge > ksel) & (thr > NEG_INF_KEY)

    @pl.when(jnp.max(tie_cols.astype(I32)) > 0)
    def _():
        need = ksel - count_ge(thr + 1)
        kpos = lax.broadcasted_iota(I32, (ck, qb), 0)

        def count_tied_below(cand):
            return count_keys(lambda blk, off: (blk == thr) & (kpos + off < cand))

        cut = _tie_cutoff(count_tied_below, need, idx_bits, cols1)

        def demote(c, carry):
            off = pl.multiple_of(c * ck, ck)
            blk = sk_ref[pl.ds(off, ck), :]
            sk_ref[pl.ds(off, ck), :] = jnp.where((blk == thr) & (kpos + off > cut), blk - 1, blk)
            return carry

        lax.fori_loop(0, nck, demote, 0)

    thr_sel = jnp.maximum(thr, NEG_INF_KEY + 1)

    m_ref[...] = jnp.full(m_ref.shape, NEG_BIG, F32)
    l_ref[...] = jnp.zeros(l_ref.shape, F32)
    acc_ref[...] = jnp.zeros(acc_ref.shape, F32)

    zero_half = jnp.zeros((qb, HEAD_DIM), BF16)
    for hh in range(B_HEADS):
        halves = [zero_half] * N_KV
        halves[hh // GROUP] = q_ref[0, hh]
        qbd_ref[hh * qb:(hh + 1) * qb, :] = jnp.concatenate(halves, axis=1)
    nq = B_HEADS * qb
    half = GROUP * qb

    def attend_chunk(c, carry):
        off = pl.multiple_of(c * ca, ca)
        bias = jnp.where(sk_ref[pl.ds(off, ca), :] >= thr_sel, 0.0, NEG_BIG)
        bias = jnp.concatenate([bias] * B_HEADS, axis=1)
        s = lax.dot_general(k_ref[0, pl.ds(off, ca), :], qbd_ref[...], _NT, preferred_element_type=F32) + bias
        m_old = m_ref[...]
        m_new = jnp.maximum(m_old, jnp.max(s, axis=0, keepdims=True))
        alpha = jnp.exp(m_old - m_new)
        p = jnp.exp(s - m_new).astype(BF16)
        pv = jnp.dot(v_ref[0, :, pl.ds(off, ca)], p, preferred_element_type=F32)
        l_ref[...] = alpha * l_ref[...] + pv[KV_W:KV_W + 1, :]
        pv = jnp.concatenate([pv[g * HEAD_DIM:(g + 1) * HEAD_DIM, g * half:(g + 1) * half] for g in range(N_KV)], axis=1)
        acc_ref[...] = alpha * acc_ref[...] + pv
        m_ref[...] = m_new
        return carry

    lax.fori_loop(0, (i * qb) // ca + 1, attend_chunk, 0)

    o = acc_ref[...] / l_ref[...]
    for hh in range(B_HEADS):
        o_ref[0, :, hh * HEAD_DIM:(hh + 1) * HEAD_DIM] = o[:, hh * qb:(hh + 1) * qb].T.astype(o_ref.dtype)


def _dsa_prompt_call(qih, wi, kib, qh, kg, vg):
    bsz, _, slen, _ = qh.shape
    qb = min(512, slen)
    ck = 512
    ksel = min(TOPK_MAX, slen // 4)
    assert slen % ck == 0 and ck >= ksel
    idx_bits = max(1, int(slen - 1).bit_length())
    ca = ck
    body = functools.partial(_dsa_prompt_body, qb=qb, ck=ck, ca=ca, ksel=ksel, idx_bits=idx_bits)
    return pl.pallas_call(
        body,
        grid=(bsz, slen // qb),
        in_specs=[
            pl.BlockSpec((1, IDX_HEADS, qb, IDX_DIM), lambda b, i: (b, 0, i, 0)),
            pl.BlockSpec((1, IDX_HEADS, qb), lambda b, i: (b, 0, i)),
            pl.BlockSpec((1, slen, IDX_DIM), lambda b, i: (b, 0, 0)),
            pl.BlockSpec((1, B_HEADS, qb, HEAD_DIM), lambda b, i: (b, 0, i, 0)),
            pl.BlockSpec((1, slen, KV_W), lambda b, i: (b, 0, 0)),
            pl.BlockSpec((1, KV_W + ONES_ROWS, slen), lambda b, i: (b, 0, 0)),
        ],
        out_specs=pl.BlockSpec((1, qb, B_W), lambda b, i: (b, i, 0)),
        out_shape=jax.ShapeDtypeStruct((bsz, slen, B_W), BF16),
        scratch_shapes=[
            pltpu.VMEM((slen, qb), I32),
            pltpu.VMEM((B_HEADS * qb, KV_W), BF16),
            pltpu.VMEM((1, B_HEADS * qb), F32),
            pltpu.VMEM((1, B_HEADS * qb), F32),
            pltpu.VMEM((HEAD_DIM, B_HEADS * qb), F32),
        ],
        compiler_params=pltpu.CompilerParams(
            dimension_semantics=("arbitrary", "arbitrary"), vmem_limit_bytes=VMEM_LIMIT),
        name="dsa_prompt",
    )(qih, wi, kib, qh, kg, vg)


def _dsa_dec_body(pt_ref, qi_ref, w_ref, q_ref, kin_ref, kn_ref, vn_ref, cki_hbm, ck_hbm, cv_hbm,
                  o_ref, kibuf, kbuf, vbuf, sems, sk_ref, *, layer, n_pages, t_dec, ksel, idx_bits):
    b = pl.program_id(0)
    past = n_pages * PAGE_SIZE
    tpad = SUBLANES
    total = past + LANES

    def page_copies(p):
        phys = pt_ref[b, p]
        dst = pl.ds(pl.multiple_of(p * PAGE_SIZE, PAGE_SIZE), PAGE_SIZE)
        return (pltpu.make_async_copy(cki_hbm.at[layer, phys], kibuf.at[:, dst], sems.at[0, p]),
                pltpu.make_async_copy(ck_hbm.at[layer, phys], kbuf.at[:, dst], sems.at[1, p]),
                pltpu.make_async_copy(cv_hbm.at[layer, phys], vbuf.at[:, dst], sems.at[2, p]))

    def start_page(p, carry):
        for cp in page_copies(p):
            cp.start()
        return carry

    def wait_page(p, carry):
        for cp in page_copies(p):
            cp.wait()
        return carry

    lax.fori_loop(0, n_pages, start_page, 0)
    lax.fori_loop(0, n_pages, wait_page, 0)

    rows1 = (tpad, 1)
    qi = qi_ref[0]
    w = w_ref[0]

    def scores(d):
        r = jnp.maximum(d, 0.0) * w
        n = r.shape[-1]
        return jnp.sum(r.reshape(IDX_HEADS, tpad, n), axis=0)

    d_past = jnp.dot(qi, kibuf[...].astype(BF16), preferred_element_type=F32)
    sk_ref[:, 0:past] = _sortable_key(scores(d_past))
    trow = lax.broadcasted_iota(I32, (tpad, LANES), 0)
    tcol = lax.broadcasted_iota(I32, (tpad, LANES), 1)
    d_new = lax.dot_general(qi, kin_ref[0], _NT, preferred_element_type=F32)
    sc_new = jnp.where((tcol <= trow) & (tcol < t_dec), scores(d_new), -jnp.inf)
    sk_ref[:, past:total] = _sortable_key(sc_new)

    def count_ge(cand):
        return jnp.sum(jnp.where(sk_ref[...] >= cand, 1, 0), axis=1, keepdims=True)

    thr = _kth_largest_key(count_ge, ksel, rows1)
    n_ge = count_ge(thr)
    real_row = lax.broadcasted_iota(I32, rows1, 0) < t_dec
    tie_rows = (n_ge > ksel) & (thr > NEG_INF_KEY) & real_row

    @pl.when(jnp.max(tie_rows.astype(I32)) > 0)
    def _():
        need = ksel - count_ge(thr + 1)
        lane = lax.broadcasted_iota(I32, (tpad, total), 1)

        def count_tied_below(cand):
            return jnp.sum(jnp.where((sk_ref[...] == thr) & (lane < cand), 1, 0), axis=1, keepdims=True)

        cut = _tie_cutoff(count_tied_below, need, idx_bits, rows1)
        blk = sk_ref[...]
        sk_ref[...] = jnp.where((blk == thr) & (lane > cut), blk - 1, blk)

    thr_sel = jnp.maximum(thr, NEG_INF_KEY + 1)
    bias = jnp.where(sk_ref[...] >= thr_sel, 0.0, NEG_BIG)

    kt_past = kbuf[...].astype(BF16)
    vt_past = vbuf[...].astype(BF16)
    for g in range(N_KV):
        qg = q_ref[0, g]
        s = jnp.concatenate([
            jnp.dot(qg, kt_past, preferred_element_type=F32),
            lax.dot_general(qg, kn_ref[0], _NT, preferred_element_type=F32)], axis=1)
        s = (s.reshape(GROUP, tpad, total) + bias[None]).reshape(GROUP * tpad, total)
        p = jnp.exp(s - jnp.max(s, axis=1, keepdims=True))
        l = jnp.sum(p, axis=1, keepdims=True)
        pb = p.astype(BF16)
        o = (lax.dot_general(pb[:, 0:past], vt_past, _NT, preferred_element_type=F32)
             + jnp.dot(pb[:, past:total], vn_ref[0], preferred_element_type=F32))
        o_ref[0, g] = (o / l)[:, g * HEAD_DIM:(g + 1) * HEAD_DIM]


def _dsa_dec_call(page_table, qi_l, w_l, q_l, ki_new, k_new, v_new, cache_kidx, cache_k, cache_v, *, layer, t_dec):
    bd, n_pages = page_table.shape
    past = n_pages * PAGE_SIZE
    total = past + LANES
    ksel = min(TOPK_MAX, (past + t_dec) // 4)
    idx_bits = int(total - 1).bit_length()
    tpad = SUBLANES
    body = functools.partial(_dsa_dec_body, layer=layer, n_pages=n_pages, t_dec=t_dec, ksel=ksel, idx_bits=idx_bits)
    any_spec = pl.BlockSpec(memory_space=pl.ANY)
    grid_spec = pltpu.PrefetchScalarGridSpec(
        num_scalar_prefetch=1,
        grid=(bd,),
        in_specs=[
            pl.BlockSpec((1, IDX_HEADS * tpad, IDX_DIM), lambda b, pt: (b, 0, 0)),
            pl.BlockSpec((1, IDX_HEADS * tpad, 1), lambda b, pt: (b, 0, 0)),
            pl.BlockSpec((1, N_KV, GROUP * tpad, KV_W), lambda b, pt: (b, 0, 0, 0)),
            pl.BlockSpec((1, LANES, IDX_DIM), lambda b, pt: (b, 0, 0)),
            pl.BlockSpec((1, LANES, KV_W), lambda b, pt: (b, 0, 0)),
            pl.BlockSpec((1, LANES, KV_W), lambda b, pt: (b, 0, 0)),
            any_spec, any_spec, any_spec,
        ],
        out_specs=pl.BlockSpec((1, N_KV, GROUP * tpad, HEAD_DIM), lambda b, pt: (b, 0, 0, 0)),
        scratch_shapes=[
            pltpu.VMEM((IDX_DIM, past), F32),
            pltpu.VMEM((KV_W, past), F32),
            pltpu.VMEM((KV_W, past), F32),
            pltpu.SemaphoreType.DMA((3, n_pages)),
            pltpu.VMEM((tpad, total), I32),
        ],
    )
    return pl.pallas_call(
        body,
        grid_spec=grid_spec,
        out_shape=jax.ShapeDtypeStruct((bd, N_KV, GROUP * tpad, HEAD_DIM), F32),
        compiler_params=pltpu.CompilerParams(dimension_semantics=("arbitrary",), vmem_limit_bytes=VMEM_LIMIT),
        name="dsa_dec",
    )(page_table, qi_l, w_l, q_l, ki_new, k_new, v_new, cache_kidx, cache_k, cache_v)


def _dsa_dec(qih, wi, qh, kf, vf, kib, page_table, cache_kidx, cache_k2, cache_v2, *, layer, bd, td):
    tpad = SUBLANES
    md = bd * td

    def head_rows(xh, nh):
        xh = xh[0].reshape(nh, bd, td, HEAD_DIM).transpose(1, 0, 2, 3)
        return jnp.pad(xh, ((0, 0), (0, 0), (0, tpad - td), (0, 0)))

    qi_l = head_rows(qih, IDX_HEADS).reshape(bd, IDX_HEADS * tpad, IDX_DIM)
    w_l = jnp.pad(wi[0].reshape(IDX_HEADS, bd, td).transpose(1, 0, 2), ((0, 0), (0, 0), (0, tpad - td)))
    w_l = w_l.reshape(bd, IDX_HEADS * tpad, 1)
    q_g = head_rows(qh, B_HEADS).reshape(bd, N_KV, GROUP * tpad, HEAD_DIM)
    zeros_q = jnp.zeros_like(q_g)
    q_l = jnp.stack([jnp.concatenate([q_g[:, 0], zeros_q[:, 0]], axis=-1),
                     jnp.concatenate([zeros_q[:, 1], q_g[:, 1]], axis=-1)], axis=1)
    pad_rows = ((0, 0), (0, LANES - td), (0, 0))
    ki_new = jnp.pad(kib[0].reshape(bd, td, IDX_DIM), pad_rows)
    k_new = jnp.pad(kf[0].astype(BF16).reshape(bd, td, KV_W), pad_rows)
    v_new = jnp.pad(vf[0].astype(BF16).reshape(bd, td, KV_W), pad_rows)
    o_l = _dsa_dec_call(page_table, qi_l, w_l, q_l, ki_new, k_new, v_new, cache_kidx, cache_k2, cache_v2,
                        layer=layer, t_dec=td)
    o_g = o_l.reshape(bd, N_KV, GROUP, tpad, HEAD_DIM)[:, :, :, :td]
    return o_g.transpose(0, 3, 1, 2, 4).reshape(1, md, B_W).astype(BF16)


FF_TILE = D_FF // 2


def _post_body(*refs, m, alpha, t_dec):
    if t_dec:
        (x_ref, a_ref, b_ref, c_ref, wo_ref, g1_ref, b1_ref, wup_ref, cf_ref, wdn_ref, g2_ref, b2_ref,
         p1_ref, p2_ref, y_ref, st_ref, hist_ref) = refs
    else:
        (x_ref, a_ref, b_ref, c_ref, wo_ref, g1_ref, b1_ref, wup_ref, cf_ref, wdn_ref, g2_ref, b2_ref,
         y_ref, st_ref, hist_ref) = refs

    @pl.when(pl.program_id(1) == 0)
    def _():
        hist_ref[0:SUBLANES, :] = jnp.zeros((SUBLANES, 2 * D_FF), F32)

    mix = (jnp.dot(a_ref[0], wo_ref[0:A_W, :], preferred_element_type=F32)
           + jnp.dot(b_ref[0], wo_ref[A_W:A_W + B_W, :], preferred_element_type=F32)
           + jnp.dot(c_ref[0], wo_ref[A_W + B_W:, :], preferred_element_type=F32))
    x1 = _layer_norm(alpha * x_ref[0] + mix, g1_ref[...], b1_ref[...])
    x1b = x1.astype(BF16)

    t = lax.broadcasted_iota(I32, (m, 1), 0) % t_dec if t_dec else None
    f = jnp.zeros((m, D_MODEL), F32)
    for j in range(D_FF // FF_TILE):
        gcols = slice(j * FF_TILE, (j + 1) * FF_TILE)
        ucols = slice(D_FF + j * FF_TILE, D_FF + (j + 1) * FF_TILE)
        halves = []
        for cols in (gcols, ucols):
            hu = jnp.dot(x1b, wup_ref[:, cols], preferred_element_type=F32)
            hist_ref[pl.ds(SUBLANES, m), cols] = hu
            state = (t, p1_ref[:, cols], p2_ref[:, cols]) if t_dec else None
            halves.append(_causal_conv3(hu, hist_ref, cols, cf_ref, m, state))
        gate, up = halves
        act = gate * (1.0 / (1.0 + jnp.exp(-gate))) * up
        f = f + jnp.dot(act.astype(BF16), wdn_ref[gcols, :], preferred_element_type=F32)
    y_ref[0] = _layer_norm(alpha * x1 + f, g2_ref[...], b2_ref[...])

    if t_dec:
        st_ref[0] = hist_ref[pl.ds(SUBLANES, m), :]
    else:
        last = hist_ref[pl.ds(SUBLANES + m - 2, 2), :]
        st_ref[0] = last
        hist_ref[SUBLANES - 2:SUBLANES, :] = last


def _post_call(x, a, bo, c, wo, g1, b1, wup, cf, wdn, g2, b2, p1=None, p2=None, *, tm, alpha, t_dec=0):
    bsz, tlen, _ = x.shape
    nt = tlen // tm

    def row_blk(width):
        return pl.BlockSpec((1, tm, width), lambda b, j: (b, j, 0))

    in_specs = [row_blk(D_MODEL), row_blk(A_W), row_blk(B_W), row_blk(C_W)] + [_vmem_full()] * 8
    args = [x, a, bo, c, wo, g1, b1, wup, cf, wdn, g2, b2]
    if t_dec:
        in_specs += [_vmem_full(), _vmem_full()]
        args += [p1, p2]
        st_shape = jax.ShapeDtypeStruct((bsz, tlen, 2 * D_FF), F32)
        st_spec = row_blk(2 * D_FF)
    else:
        st_shape = jax.ShapeDtypeStruct((bsz, CONV_W - 1, 2 * D_FF), F32)
        st_spec = pl.BlockSpec((1, CONV_W - 1, 2 * D_FF), lambda b, j: (b, 0, 0))
    return pl.pallas_call(
        functools.partial(_post_body, m=tm, alpha=alpha, t_dec=t_dec),
        grid=(bsz, nt),
        in_specs=in_specs,
        out_specs=[row_blk(D_MODEL), st_spec],
        out_shape=[jax.ShapeDtypeStruct((bsz, tlen, D_MODEL), F32), st_shape],
        scratch_shapes=[pltpu.VMEM((SUBLANES + tm, 2 * D_FF), F32)],
        compiler_params=pltpu.CompilerParams(
            dimension_semantics=("arbitrary", "arbitrary"), vmem_limit_bytes=VMEM_LIMIT),
        name="post_dec" if t_dec else "post",
    )(*args)


def _pad_cols(w, width):
    return jnp.pad(w, ((0, 0), (0, width - w.shape[1])))


def _rope_tables(pos):
    inv = ROPE_THETA ** (-jnp.arange(ROT_HALF, dtype=F32) / ROT_HALF)
    ang = pos.astype(F32)[:, None] * inv[None, :]
    cos, sin = jnp.cos(ang), jnp.sin(ang)
    n = pos.shape[0]
    rest = HEAD_DIM - ROT_DIM
    cos_h = jnp.concatenate([cos, cos, jnp.ones((n, rest), F32)], axis=1)
    sin_h = jnp.concatenate([-sin, sin, jnp.zeros((n, rest), F32)], axis=1)
    reps = LANES // HEAD_DIM
    return jnp.tile(cos_h, (1, reps)), jnp.tile(sin_h, (1, reps))


def kernel(x_prompt, x_sample, cache_k, cache_v, cache_kidx, page_table, state_conv, state_ffn_conv, w_in, conv_a,
           ln_cv_g, ln_cv_b, w_sp, b_sp, w_o, ln1_g, ln1_b, w_up, conv_f, w_down, ln2_g, ln2_b):
    depth = w_in.shape[0]
    alpha = (2 * depth) ** 0.25
    bp, sp, _ = x_prompt.shape
    bd, td, _ = x_sample.shape
    n_pages = page_table.shape[1]
    past = n_pages * PAGE_SIZE
    md = bd * td
    tpad = SUBLANES
    assert sp % 512 == 0 and md % CHUNK == 0 and CHUNK % td == 0 and td <= tpad and td <= CHUNK
    tm_proj = 512
    tm_post = 256

    cos_p, sin_p = _rope_tables(jnp.arange(sp, dtype=I32))
    cos_s, sin_s = _rope_tables(jnp.tile(past + jnp.arange(td, dtype=I32), bd))

    n_pool = cache_k.shape[1]
    cache_k2 = cache_k.transpose(0, 1, 3, 4, 2).reshape(depth, n_pool, KV_W, PAGE_SIZE)
    cache_v2 = cache_v.transpose(0, 1, 3, 4, 2).reshape(depth, n_pool, KV_W, PAGE_SIZE)
    cache_kidx = cache_kidx.transpose(0, 1, 3, 2)

    tril = jnp.tril(jnp.ones((CHUNK, CHUNK), bool))
    eye_b = jnp.eye(CHUNK // td, dtype=F32)
    t_of_row = jnp.arange(md) % td

    hp, hs = x_prompt, x_sample.reshape(1, md, D_MODEL)
    outs = [[] for _ in range(11)]
    for l in range(depth):
        w = w_in[l]
        w_pad = jnp.concatenate([
            w[:, :OFF_KI],
            _pad_cols(w[:, OFF_KI:OFF_KI + IDX_DIM], LANES),
            _pad_cols(w[:, OFF_KI + IDX_DIM:OFF_KI + IDX_DIM + IDX_HEADS], LANES),
            w[:, OFF_KI + IDX_DIM + IDX_HEADS:]], axis=1).astype(BF16)
        lng, lnb = ln_cv_g[l][None], ln_cv_b[l][None]
        wo_b, wup_b, wdn_b = w_o[l].astype(BF16), w_up[l].astype(BF16), w_down[l].astype(BF16)
        g1, b1, g2, b2 = ln1_g[l][None], ln1_b[l][None], ln2_g[l][None], ln2_b[l][None]

        wm_p = jnp.where(tril[None], w_sp[l], 0).astype(BF16)
        bsr_p = jnp.repeat(b_sp[l].T, HEAD_DIM, axis=1)
        (a_o, c_o, qh, qih, wi, kf, vf, kif, kg, vg, kib, conv_new) = _proj_call(
            hp, w_pad, cos_p, sin_p, conv_a[l], lng, lnb, wm_p, bsr_p, tm=tm_proj)
        b_o = _dsa_prompt_call(qih, wi, kib, qh, kg, vg)
        hp, ffn_new = _post_call(hp, a_o, b_o, c_o, wo_b, g1, b1, wup_b, conv_f[l], wdn_b, g2, b2,
                                 tm=tm_post, alpha=alpha)
        outs[0].append(kf.reshape(bp, sp, N_KV, HEAD_DIM))
        outs[2].append(vf.reshape(bp, sp, N_KV, HEAD_DIM))
        outs[4].append(kif)
        outs[6].append(conv_new)
        outs[8].append(ffn_new)

        ws_small = jnp.where(tril[None, :td, :td], w_sp[l][:, :td, :td], 0)
        wm_s = jnp.einsum("ab,hts->hatbs", eye_b, ws_small).reshape(C_HEADS, CHUNK, CHUNK).astype(BF16)
        bsr_s = jnp.repeat(b_sp[l][:, :td].T, HEAD_DIM, axis=1)[jnp.arange(CHUNK) % td]
        st = state_conv[l]
        p1 = jnp.repeat(st[:, 1], td, axis=0)
        p2 = jnp.where((t_of_row == 0)[:, None], jnp.repeat(st[:, 0], td, axis=0), jnp.repeat(st[:, 1], td, axis=0))
        (a_o, c_o, qh, qih, wi, kf, vf, kif, kg, vg, kib, z_all, vn_all) = _proj_call(
            hs, w_pad, cos_s, sin_s, conv_a[l], lng, lnb, wm_s, bsr_s, p1, p2, tm=md, t_dec=td)

        b_o = _dsa_dec(qih, wi, qh, kf, vf, kib, page_table, cache_kidx, cache_k2, cache_v2, layer=l, bd=bd, td=td)

        fs = state_ffn_conv[l]
        f1 = jnp.repeat(fs[:, 1], td, axis=0)
        f2 = jnp.where((t_of_row == 0)[:, None], jnp.repeat(fs[:, 0], td, axis=0), jnp.repeat(fs[:, 1], td, axis=0))
        hs, ffn_all = _post_call(hs, a_o, b_o, c_o, wo_b, g1, b1, wup_b, conv_f[l], wdn_b, g2, b2, f1, f2,
                                 tm=md, alpha=alpha, t_dec=td)
        outs[1].append(kf.reshape(bd, td, N_KV, HEAD_DIM))
        outs[3].append(vf.reshape(bd, td, N_KV, HEAD_DIM))
        outs[5].append(kif.reshape(bd, td, IDX_DIM))
        outs[7].append(z_all.reshape(bd, td, A_W)[:, td - (CONV_W - 1):])
        outs[9].append(ffn_all.reshape(bd, td, 2 * D_FF)[:, td - (CONV_W - 1):])
        outs[10].append(vn_all.reshape(bd, td, C_W))

    return (hp, hs.reshape(bd, td, D_MODEL)) + tuple(jnp.stack(o) for o in outs)
```
